```python
import math
import jax, jax.numpy as jnp
from jax import lax
import numpy as np

D_MODEL = 2048
BATCH = 8
SEQ = 2048
DEPTH = 4

CHUNK = 64
MIX_WIDTH = D_MODEL
GLA_WIDTH = D_MODEL // 2
GLA_HEADS = 4
GLA_KEY_WIDTH = GLA_WIDTH // 2
GLA_DK = GLA_KEY_WIDTH // GLA_HEADS
GLA_DV = GLA_WIDTH // GLA_HEADS
GLA_GATE_RANK = 16
GLA_GATE_TAU = 16.0
CONV_WIDTH = MIX_WIDTH - GLA_WIDTH
CONV_K = 3
IN_PROJ_WIDTH = 2 * GLA_KEY_WIDTH + 2 * GLA_WIDTH + GLA_GATE_RANK + 3 * CONV_WIDTH
PEER_HEADS = 8
N_KEYS = 128
N_EXPERTS = N_KEYS * N_KEYS
PEER_TOPK = 16
PEER_QDIM = 256
PEER_HALF = PEER_QDIM // 2
PEER_TOKEN_BLOCK = 128
DEEPNORM_ALPHA = (2.0 * DEPTH) ** 0.25
DEEPNORM_BETA = (8.0 * DEPTH) ** -0.25
EPS = 1e-5

kernel_name = "hybrid_gla_shortconv_peer_deepnorm_adaln"


def layer_norm(x, g, b):
    xf = x.astype(jnp.float32)
    mu = jnp.mean(xf, axis=-1, keepdims=True)
    var = jnp.mean(jnp.square(xf - mu), axis=-1, keepdims=True)
    y = (xf - mu) * lax.rsqrt(var + EPS) * g.astype(jnp.float32) + b.astype(jnp.float32)
    return y.astype(x.dtype)


def rms_norm(x, g):
    xf = x.astype(jnp.float32)
    y = xf * lax.rsqrt(jnp.mean(jnp.square(xf), axis=-1, keepdims=True) + EPS) * g.astype(jnp.float32)
    return y.astype(x.dtype)


def gla_chunk_state(q, k, v, a_lr, w_gate2, b_gate):
    bsz, seq, _ = q.shape
    nc = seq // CHUNK
    f32 = jnp.float32
    qf = q.astype(f32).reshape(bsz, nc, CHUNK, GLA_HEADS, GLA_DK) * (GLA_DK ** -0.5)
    kf = k.astype(f32).reshape(bsz, nc, CHUNK, GLA_HEADS, GLA_DK)
    vf = v.astype(f32).reshape(bsz, nc, CHUNK, GLA_HEADS, GLA_DV)
    gate_logit = jnp.einsum('bsr,rk->bsk', a_lr.astype(f32), w_gate2.astype(f32)) + b_gate.astype(f32)
    log_a = (jax.nn.log_sigmoid(gate_logit) / GLA_GATE_TAU).reshape(bsz, nc, CHUNK, GLA_HEADS, GLA_DK)
    cum = jnp.cumsum(log_a, axis=2)
    total = cum[:, :, -1]
    k_dec = kf * jnp.exp(total[:, :, None] - cum)

    def step(state, xs):
        dec, kc, vc, qc = xs
        state = jnp.exp(dec)[..., None] * state + jnp.einsum('blhk,blhv->bhkv', kc, vc)
        return state, jnp.einsum('blhk,bhkv->blhv', qc, state)

    xs = (jnp.moveaxis(total, 1, 0), jnp.moveaxis(k_dec, 1, 0),
          jnp.moveaxis(vf, 1, 0), jnp.moveaxis(qf, 1, 0))
    init = jnp.zeros((bsz, GLA_HEADS, GLA_DK, GLA_DV), f32)
    _, o = lax.scan(step, init, xs)
    return jnp.moveaxis(o, 0, 1).reshape(bsz, seq, GLA_HEADS, GLA_DV)


def token_mixer(h, w_in, w_gate2, b_gate, gla_norm_g, conv_w, conv_norm_g, w_out):
    bsz, seq, _ = h.shape
    proj = jnp.einsum('bsd,de->bse', h, w_in)
    widths = [GLA_KEY_WIDTH, GLA_KEY_WIDTH, GLA_WIDTH, GLA_WIDTH, GLA_GATE_RANK,
              CONV_WIDTH, CONV_WIDTH, CONV_WIDTH]
    split_at = np.cumsum(widths)[:-1].tolist()
    q, k, v, r, a_lr, cb, cc, ch = jnp.split(proj, split_at, axis=-1)

    o = gla_chunk_state(q, k, v, a_lr, w_gate2, b_gate).astype(h.dtype)
    o = rms_norm(o, gla_norm_g) * jax.nn.silu(r.reshape(bsz, seq, GLA_HEADS, GLA_DV))
    y_gla = o.reshape(bsz, seq, GLA_WIDTH)

    u = cc * ch
    up = jnp.pad(u, ((0, 0), (CONV_K - 1, 0), (0, 0)))
    conv = conv_w[0] * up[:, :-2] + conv_w[1] * up[:, 1:-1] + conv_w[2] * up[:, 2:]
    y_conv = rms_norm(cb * conv, conv_norm_g)

    y = jnp.concatenate([y_gla, y_conv], axis=-1)
    return jnp.einsum('bse,ed->bsd', y, w_out)


def peer_ffn(h, wq, keys, u_tab, v_tab):
    bsz, seq, dm = h.shape
    t = bsz * seq
    xt = h.reshape(t, dm)
    q = jnp.einsum('td,de->te', xt, wq).reshape(t, PEER_HEADS, 2, PEER_HALF)
    s = jnp.einsum('thpd,hpnd->thpn', q, keys).astype(jnp.float32)
    sv, si = lax.top_k(s, PEER_TOPK)
    cand = sv[:, :, 0, :, None] + sv[:, :, 1, None, :]
    cv, ci = lax.top_k(cand.reshape(t, PEER_HEADS, PEER_TOPK * PEER_TOPK), PEER_TOPK)
    e1 = jnp.take_along_axis(si[:, :, 0], ci // PEER_TOPK, axis=-1)
    e2 = jnp.take_along_axis(si[:, :, 1], ci % PEER_TOPK, axis=-1)
    experts = e1 * N_KEYS + e2
    gates = jax.nn.softmax(cv, axis=-1).astype(h.dtype)

    nb = t // PEER_TOKEN_BLOCK

    def block(args):
        xb, eb, gb = args
        a = jnp.einsum('thkd,td->thk', u_tab[eb], xb)
        return jnp.einsum('thk,thkd->td', gb * jax.nn.gelu(a), v_tab[eb])

    y = lax.map(block, (xt.reshape(nb, PEER_TOKEN_BLOCK, dm),
                        experts.reshape(nb, PEER_TOKEN_BLOCK, PEER_HEADS, PEER_TOPK),
                        gates.reshape(nb, PEER_TOKEN_BLOCK, PEER_HEADS, PEER_TOPK)))
    return y.reshape(bsz, seq, dm)


def setup_inputs(seed: int = 0) -> dict:
    key = jax.random.key(seed)
    ks = jax.random.split(key, 20)
    nrm = lambda k, shape, s: jax.random.normal(k, shape, jnp.float32) * s
    L, D = DEPTH, D_MODEL
    return {
        "x": nrm(ks[0], (BATCH, SEQ, D), 1.0),
        "c": nrm(ks[1], (BATCH, D), 1.0),
        "ada_w": nrm(ks[2], (L, D, 6 * D), 0.2 * D ** -0.5),
        "ada_b": nrm(ks[3], (L, 6 * D), 0.01),
        "w_in": nrm(ks[4], (L, D, IN_PROJ_WIDTH), D ** -0.5),
        "w_gate2": nrm(ks[5], (L, GLA_GATE_RANK, GLA_KEY_WIDTH), GLA_GATE_RANK ** -0.5),
        "b_gate": nrm(ks[6], (L, GLA_KEY_WIDTH), 0.1),
        "gla_norm_g": 1.0 + nrm(ks[7], (L, GLA_DV), 0.02),
        "conv_w": nrm(ks[8], (L, CONV_K, CONV_WIDTH), CONV_K ** -0.5),
        "conv_norm_g": 1.0 + nrm(ks[9], (L, CONV_WIDTH), 0.02),
        "w_out": nrm(ks[10], (L, MIX_WIDTH, D), DEEPNORM_BETA * MIX_WIDTH ** -0.5),
        "ln1_g": 1.0 + nrm(ks[11], (L, D), 0.02),
        "ln1_b": nrm(ks[12], (L, D), 0.02),
        "peer_wq": nrm(ks[13], (L, D, PEER_HEADS * PEER_QDIM), D ** -0.5),
        "peer_keys": nrm(ks[14], (L, PEER_HEADS, 2, N_KEYS, PEER_HALF), PEER_HALF ** -0.5),
        "peer_u": nrm(ks[15], (L, N_EXPERTS, D), D ** -0.5),
        "peer_v": nrm(ks[16], (L, N_EXPERTS, D), DEEPNORM_BETA * PEER_HEADS ** -0.5),
        "ln2_g": 1.0 + nrm(ks[17], (L, D), 0.02),
        "ln2_b": nrm(ks[18], (L, D), 0.02),
    }


def reference(x, c, ada_w, ada_b, w_in, w_gate2, b_gate, gla_norm_g, conv_w, conv_norm_g,
              w_out, ln1_g, ln1_b, peer_wq, peer_keys, peer_u, peer_v, ln2_g, ln2_b):
    c_act = jax.nn.silu(c)
    for l in range(DEPTH):
        mod = jnp.einsum('bd,de->be', c_act, ada_w[l]) + ada_b[l]
        sh1, sc1, g1, sh2, sc2, g2 = jnp.split(mod[:, None, :], 6, axis=-1)
        h = x * (1.0 + sc1) + sh1
        mix = token_mixer(h, w_in[l], w_gate2[l], b_gate[l], gla_norm_g[l], conv_w[l],
                          conv_norm_g[l], w_out[l])
        x = layer_norm(DEEPNORM_ALPHA * x + (1.0 + g1) * mix, ln1_g[l], ln1_b[l])
        h = x * (1.0 + sc2) + sh2
        ffn = peer_ffn(h, peer_wq[l], peer_keys[l], peer_u[l], peer_v[l])
        x = layer_norm(DEEPNORM_ALPHA * x + (1.0 + g2) * ffn, ln2_g[l], ln2_b[l])
    return x
```

```python
import functools
import math

import jax
import jax.numpy as jnp
from jax import lax
from jax.experimental import pallas as pl
from jax.experimental.pallas import tpu as pltpu

CHUNK = 64
GLA_HEADS = 4
GLA_DK = 128
GLA_DV = 256
GLA_KEY_WIDTH = GLA_HEADS * GLA_DK
GLA_WIDTH = GLA_HEADS * GLA_DV
GLA_GATE_RANK = 16
GLA_GATE_TAU = 16.0
CONV_WIDTH = 1024
PEER_HEADS = 8
N_KEYS = 128
PEER_TOPK = 16
PEER_HALF = 128
EPS = 1e-5

LANES = 128
SUBLANES = 8
VMEM_LIMIT_BYTES = 56 * 1024 * 1024

ALR_PAD = LANES
PROJ_WIDTH = 2 * GLA_KEY_WIDTH + 2 * GLA_WIDTH + 3 * CONV_WIDTH + ALR_PAD

F32 = jnp.float32
BF16 = jnp.bfloat16
NEG_INF = float("-inf")


def _cparams(*sem):
    return pltpu.CompilerParams(dimension_semantics=sem, vmem_limit_bytes=VMEM_LIMIT_BYTES)


def _layer_norm(z, g, b):
    mu = jnp.mean(z, axis=-1, keepdims=True)
    zc = z - mu
    var = jnp.mean(zc * zc, axis=-1, keepdims=True)
    return zc * lax.rsqrt(var + EPS) * g + b


def _rms_norm(z, g):
    return z * lax.rsqrt(jnp.mean(z * z, axis=-1, keepdims=True) + EPS) * g


def _sigmoid(z):
    return 1.0 / (1.0 + jnp.exp(-z))


def _ada_kernel(c_ref, w_ref, b_ref, o_ref):
    c = c_ref[...]
    c_act = (c * _sigmoid(c)).astype(BF16)
    o_ref[...] = jnp.dot(c_act, w_ref[...].astype(BF16), preferred_element_type=F32) + b_ref[...]


def _ada_mod(c, ada_w, ada_b):
    depth, d, n = ada_w.shape
    bsz = c.shape[0]
    tn = 1024
    return pl.pallas_call(
        _ada_kernel,
        out_shape=jax.ShapeDtypeStruct((depth, bsz, n), F32),
        grid=(depth, n // tn),
        in_specs=[
            pl.BlockSpec((bsz, d), lambda l, j: (0, 0)),
            pl.BlockSpec((None, d, tn), lambda l, j: (l, 0, j)),
            pl.BlockSpec((None, 1, tn), lambda l, j: (l, 0, j)),
        ],
        out_specs=pl.BlockSpec((None, bsz, tn), lambda l, j: (l, 0, j)),
        compiler_params=_cparams("arbitrary", "arbitrary"),
        name="ada_mod",
    )(c, ada_w, ada_b.reshape(depth, 1, n))


def _in_proj_kernel(x_ref, mod_ref, w_ref, o_ref, h_ref):
    @pl.when(pl.program_id(1) == 0)
    def _():
        sh = mod_ref[0:1, :]
        sc = mod_ref[1:2, :]
        h_ref[...] = (x_ref[...] * (1.0 + sc) + sh).astype(BF16)

    o_ref[...] = jnp.dot(h_ref[...], w_ref[...], preferred_element_type=F32)


def _in_proj(x2, mod_l, w_in_p, seq):
    t, d = x2.shape
    n = w_in_p.shape[1]
    tm, tn = 512, 896
    per_b = seq // tm
    return pl.pallas_call(
        _in_proj_kernel,
        out_shape=jax.ShapeDtypeStruct((t, n), F32),
        grid=(t // tm, n // tn),
        in_specs=[
            pl.BlockSpec((tm, d), lambda i, j: (i, 0)),
            pl.BlockSpec((None, 6, d), lambda i, j: (i // per_b, 0, 0)),
            pl.BlockSpec((d, tn), lambda i, j: (0, j)),
        ],
        out_specs=pl.BlockSpec((tm, tn), lambda i, j: (i, j)),
        scratch_shapes=[pltpu.VMEM((tm, d), BF16)],
        compiler_params=_cparams("arbitrary", "arbitrary"),
        name="in_proj",
    )(x2, mod_l, w_in_p)


GLA_ROWS = 256


def _gla_conv_kernel(q_ref, k_ref, v_ref, r_ref, cb_ref, cc_ref, ch_ref, alr_ref,
                     wg_ref, bg_ref, gng_ref, cw_ref, cng_ref, y_ref, state_ref, carry_ref):
    @pl.when(pl.program_id(1) == 0)
    def _():
        state_ref[...] = jnp.zeros_like(state_ref)
        carry_ref[...] = jnp.zeros_like(carry_ref)

    logit = jnp.dot(alr_ref[...], wg_ref[...], preferred_element_type=F32,
                    precision=lax.Precision.HIGHEST) + bg_ref[...]
    log_a = (jnp.minimum(logit, 0.0) - jnp.log(1.0 + jnp.exp(-jnp.abs(logit)))) * (1.0 / GLA_GATE_TAU)
    row = lax.broadcasted_iota(jnp.int32, (CHUNK, CHUNK), 0)
    col = lax.broadcasted_iota(jnp.int32, (CHUNK, CHUNK), 1)
    tri = (col <= row).astype(F32)
    gng = gng_ref[...]
    scale = GLA_DK ** -0.5
    for c in range(GLA_ROWS // CHUNK):
        rows = slice(c * CHUNK, (c + 1) * CHUNK)
        cum = jnp.dot(tri, log_a[rows], preferred_element_type=F32, precision=lax.Precision.HIGHEST)
        total = cum[CHUNK - 1:CHUNK, :]
        k_dec = (k_ref[rows, :] * jnp.exp(total - cum)).astype(BF16)
        dec = jnp.exp(total)
        qs = (q_ref[rows, :] * scale).astype(BF16)
        for h in range(GLA_HEADS):
            ks = slice(h * GLA_DK, (h + 1) * GLA_DK)
            vs = slice(h * GLA_DV, (h + 1) * GLA_DV)
            vh = v_ref[rows, vs].astype(BF16)
            upd = lax.dot_general(vh, k_dec[:, ks], (((0,), (0,)), ((), ())), preferred_element_type=F32)
            st = state_ref[h] * dec[:, ks] + upd
            state_ref[h] = st
            o = lax.dot_general(qs[:, ks], st.astype(BF16), (((1,), (1,)), ((), ())),
                                preferred_element_type=F32)
            rh = r_ref[rows, vs]
            y_ref[rows, vs] = (_rms_norm(o, gng) * (rh * _sigmoid(rh))).astype(y_ref.dtype)

    u = cc_ref[...] * ch_ref[...]
    prev = carry_ref[...]
    ridx = lax.broadcasted_iota(jnp.int32, (GLA_ROWS, 1), 0)
    u1 = jnp.where(ridx == 0, prev[SUBLANES - 1:SUBLANES, :], pltpu.roll(u, 1, 0))
    u2 = jnp.where(ridx == 0, prev[SUBLANES - 2:SUBLANES - 1, :],
                   jnp.where(ridx == 1, prev[SUBLANES - 1:SUBLANES, :], pltpu.roll(u, 2, 0)))
    conv = cw_ref[0:1, :] * u2 + cw_ref[1:2, :] * u1 + cw_ref[2:3, :] * u
    carry_ref[...] = u[GLA_ROWS - SUBLANES:, :]
    y_ref[:, GLA_WIDTH:] = _rms_norm(cb_ref[...] * conv, cng_ref[...]).astype(y_ref.dtype)


def _gla_conv(proj, wg_p, bg, gng, cw, cng, bsz, seq):
    t = proj.shape[0]
    nblk = seq // GLA_ROWS
    rmap = lambda width_blocks: (lambda b, j: (b * nblk + j, width_blocks))
    kw, vw = GLA_KEY_WIDTH, GLA_WIDTH
    const = lambda b, j: (0, 0)
    return pl.pallas_call(
        _gla_conv_kernel,
        out_shape=jax.ShapeDtypeStruct((t, GLA_WIDTH + CONV_WIDTH), BF16),
        grid=(bsz, nblk),
        in_specs=[
            pl.BlockSpec((GLA_ROWS, kw), rmap(0)),
            pl.BlockSpec((GLA_ROWS, kw), rmap(1)),
            pl.BlockSpec((GLA_ROWS, vw), rmap(1)),
            pl.BlockSpec((GLA_ROWS, vw), rmap(2)),
            pl.BlockSpec((GLA_ROWS, CONV_WIDTH), rmap(3)),
            pl.BlockSpec((GLA_ROWS, CONV_WIDTH), rmap(4)),
            pl.BlockSpec((GLA_ROWS, CONV_WIDTH), rmap(5)),
            pl.BlockSpec((GLA_ROWS, ALR_PAD), rmap(6144 // ALR_PAD)),
            pl.BlockSpec((ALR_PAD, kw), const),
            pl.BlockSpec((1, kw), const),
            pl.BlockSpec((1, GLA_DV), const),
            pl.BlockSpec((3, CONV_WIDTH), const),
            pl.BlockSpec((1, CONV_WIDTH), const),
        ],
        out_specs=pl.BlockSpec((GLA_ROWS, GLA_WIDTH + CONV_WIDTH), lambda b, j: (b * nblk + j, 0)),
        scratch_shapes=[pltpu.VMEM((GLA_HEADS, GLA_DV, GLA_DK), F32),
                        pltpu.VMEM((SUBLANES, CONV_WIDTH), F32)],
        compiler_params=_cparams("arbitrary", "arbitrary"),
        name="gla_conv",
    )(proj, proj, proj, proj, proj, proj, proj, proj, wg_p, bg, gng, cw, cng)


def _out_proj_ln_kernel(alpha, y_ref, x_ref, mod_ref, w_ref, g_ref, b_ref, o_ref):
    mix = jnp.dot(y_ref[...], w_ref[...], preferred_element_type=F32)
    g1 = mod_ref[2:3, :]
    o_ref[...] = _layer_norm(alpha * x_ref[...] + (1.0 + g1) * mix, g_ref[...], b_ref[...])


def _out_proj_ln(y, x2, mod_l, w_out, ln_g, ln_b, seq, alpha):
    t, d = x2.shape
    tm = 512
    per_b = seq // tm
    const = lambda i: (0, 0)
    return pl.pallas_call(
        functools.partial(_out_proj_ln_kernel, alpha),
        out_shape=jax.ShapeDtypeStruct((t, d), F32),
        grid=(t // tm,),
        in_specs=[
            pl.BlockSpec((tm, y.shape[1]), lambda i: (i, 0)),
            pl.BlockSpec((tm, d), lambda i: (i, 0)),
            pl.BlockSpec((None, 6, d), lambda i: (i // per_b, 0, 0)),
            pl.BlockSpec(w_out.shape, const),
            pl.BlockSpec((1, d), const),
            pl.BlockSpec((1, d), const),
        ],
        out_specs=pl.BlockSpec((tm, d), lambda i: (i, 0)),
        compiler_params=_cparams("arbitrary"),
        name="out_proj_ln",
    )(y, x2, mod_l, w_out, ln_g, ln_b)


def _peer_scores_kernel(x_ref, mod_ref, wq_ref, keys_ref, s_ref):
    sh = mod_ref[3:4, :]
    sc = mod_ref[4:5, :]
    h = (x_ref[...] * (1.0 + sc) + sh).astype(BF16)
    q = jnp.dot(h, wq_ref[...], preferred_element_type=F32).astype(BF16)
    for g in range(2 * PEER_HEADS):
        qg = q[:, g * PEER_HALF:(g + 1) * PEER_HALF]
        s_ref[g] = lax.dot_general(keys_ref[g], qg, (((1,), (1,)), ((), ())), preferred_element_type=F32)


def _peer_scores(x2, mod_l, wq, keys_g, seq):
    t, d = x2.shape
    tm = 512
    per_b = seq // tm
    ng = 2 * PEER_HEADS
    return pl.pallas_call(
        _peer_scores_kernel,
        out_shape=jax.ShapeDtypeStruct((ng, N_KEYS, t), F32),
        grid=(t // tm,),
        in_specs=[
            pl.BlockSpec((tm, d), lambda i: (i, 0)),
            pl.BlockSpec((None, 6, d), lambda i: (i // per_b, 0, 0)),
            pl.BlockSpec(wq.shape, lambda i: (0, 0)),
            pl.BlockSpec(keys_g.shape, lambda i: (0, 0, 0)),
        ],
        out_specs=pl.BlockSpec((ng, N_KEYS, tm), lambda i: (0, 0, i)),
        compiler_params=_cparams("arbitrary"),
        name="peer_scores",
    )(x2, mod_l, wq, keys_g)


TOPK_TOKENS = 128


def _top16_rows(s, n_rows):
    iota = lax.broadcasted_iota(jnp.int32, s.shape, 0)
    vals, ids = [], []
    for _ in range(PEER_TOPK):
        m = jnp.max(s, axis=0, keepdims=True)
        idx = jnp.min(jnp.where(s == m, iota, n_rows), axis=0, keepdims=True)
        vals.append(m)
        ids.append(idx)
        s = jnp.where(iota == idx, NEG_INF, s)
    return jnp.concatenate(vals, axis=0), jnp.concatenate(ids, axis=0)


def _select_rows(table, sel):
    out = jnp.zeros(sel.shape, table.dtype)
    for a in range(PEER_TOPK):
        out = jnp.where(sel == a, table[a:a + 1, :], out)
    return out


def _peer_topk_kernel(s_ref, e_ref, g_ref, et_ref, gt_ref):
    def head(hd, carry):
        v1, i1 = _top16_rows(s_ref[2 * hd], N_KEYS)
        v2, i2 = _top16_rows(s_ref[2 * hd + 1], N_KEYS)
        cand = jnp.concatenate([v1[a:a + 1, :] + v2 for a in range(PEER_TOPK)], axis=0)
        cv, ci = _top16_rows(cand, PEER_TOPK * PEER_TOPK)
        e1 = _select_rows(i1, ci // PEER_TOPK)
        e2 = _select_rows(i2, ci % PEER_TOPK)
        p = jnp.exp(cv - cv[0:1, :])
        gates = p / jnp.sum(p, axis=0, keepdims=True)
        rows = pl.ds(pl.multiple_of(hd * PEER_TOPK, PEER_TOPK), PEER_TOPK)
        et_ref[rows, :] = e1 * N_KEYS + e2
        gt_ref[rows, :] = gates
        return carry

    lax.fori_loop(0, PEER_HEADS, head, 0)
    e_ref[...] = et_ref[...].T
    g_ref[...] = gt_ref[...].T


def _peer_topk(s):
    ng, nk, t = s.shape
    tt = TOPK_TOKENS
    width = PEER_HEADS * PEER_TOPK
    return pl.pallas_call(
        _peer_topk_kernel,
        out_shape=(jax.ShapeDtypeStruct((t, width), jnp.int32), jax.ShapeDtypeStruct((t, width), F32)),
        grid=(t // tt,),
        in_specs=[pl.BlockSpec((ng, nk, tt), lambda i: (0, 0, i))],
        out_specs=(pl.BlockSpec((tt, width), lambda i: (i, 0)), pl.BlockSpec((tt, width), lambda i: (i, 0))),
        scratch_shapes=[pltpu.VMEM((width, tt), jnp.int32), pltpu.VMEM((width, tt), F32)],
        compiler_params=_cparams("arbitrary"),
        name="peer_topk",
    )(s)


PEER_TOKENS = 8
PEER_PICKS = PEER_HEADS * PEER_TOPK


def _gelu_tanh(z):
    return 0.5 * z * (1.0 + jnp.tanh(math.sqrt(2.0 / math.pi) * (z + 0.044715 * (z * z * z))))


def _peer_mix_kernel(alpha, e_cur_ref, e_nxt_ref, gates_ref, x_ref, mod_ref, g_ref, b_ref, uv_ref,
                     o_ref, buf_ref, ffn_ref, sem_ref):
    i = pl.program_id(0)
    n = pl.num_programs(0)
    d = x_ref.shape[1]
    rows = PEER_TOKENS * PEER_PICKS

    def row_copy(e, slot, r):
        return pltpu.make_async_copy(uv_ref.at[pl.ds(e, 1), :], buf_ref.at[slot, pl.ds(r, 1), :],
                                     sem_ref.at[slot])

    def issue(e_ref, slot):
        def tok(t, carry):
            for k in range(PEER_PICKS):
                row_copy(e_ref[t, k], slot, t * PEER_PICKS + k).start()
            return carry
        lax.fori_loop(0, PEER_TOKENS, tok, 0)

    slot = lax.rem(i, 2)

    @pl.when(i == 0)
    def _():
        issue(e_cur_ref, 0)

    @pl.when(i + 1 < n)
    def _():
        issue(e_nxt_ref, 1 - slot)

    pltpu.make_async_copy(uv_ref.at[pl.ds(0, rows), :], buf_ref.at[slot], sem_ref.at[slot]).wait()

    x = x_ref[...]
    sh = mod_ref[3:4, :]
    sc = mod_ref[4:5, :]
    g2 = mod_ref[5:6, :]
    h = x * (1.0 + sc) + sh
    eye = (lax.broadcasted_iota(jnp.int32, (PEER_PICKS, PEER_PICKS), 0)
           == lax.broadcasted_iota(jnp.int32, (PEER_PICKS, PEER_PICKS), 1)).astype(F32)
    for t in range(PEER_TOKENS):
        prow = pl.ds(t * PEER_PICKS, PEER_PICKS)
        a = jnp.sum(buf_ref[slot, prow, 0:d] * h[t:t + 1, :], axis=1, keepdims=True)
        gate = jnp.sum(eye * gates_ref[t:t + 1, :], axis=1, keepdims=True)
        w = gate * _gelu_tanh(a)
        ffn_ref[t:t + 1, :] = jnp.sum(w * buf_ref[slot, prow, d:2 * d], axis=0, keepdims=True)
    o_ref[...] = _layer_norm(alpha * x + (1.0 + g2) * ffn_ref[...], g_ref[...], b_ref[...])


def _peer_mix(experts, gates, x2, mod_l, ln_g, ln_b, uv, seq, alpha):
    t, d = x2.shape
    tb = PEER_TOKENS
    nblk = t // tb
    per_b = seq // tb
    const = lambda i: (0, 0)
    return pl.pallas_call(
        functools.partial(_peer_mix_kernel, alpha),
        out_shape=jax.ShapeDtypeStruct((t, d), F32),
        grid=(nblk,),
        in_specs=[
            pl.BlockSpec((tb, PEER_PICKS), lambda i: (i, 0), memory_space=pltpu.SMEM),
            pl.BlockSpec((tb, PEER_PICKS), lambda i: (jnp.minimum(i + 1, nblk - 1), 0),
                         memory_space=pltpu.SMEM),
            pl.BlockSpec((tb, PEER_PICKS), lambda i: (i, 0)),
            pl.BlockSpec((tb, d), lambda i: (i, 0)),
            pl.BlockSpec((None, 6, d), lambda i: (i // per_b, 0, 0)),
            pl.BlockSpec((1, d), const),
            pl.BlockSpec((1, d), const),
            pl.BlockSpec(memory_space=pl.ANY),
        ],
        out_specs=pl.BlockSpec((tb, d), lambda i: (i, 0)),
        scratch_shapes=[pltpu.VMEM((2, tb * PEER_PICKS, 2 * d), F32),
                        pltpu.VMEM((tb, d), F32),
                        pltpu.SemaphoreType.DMA((2,))],
        compiler_params=_cparams("arbitrary"),
        name="peer_mix",
    )(experts, experts, gates, x2, mod_l, ln_g, ln_b, uv)


def kernel(x, c, ada_w, ada_b, w_in, w_gate2, b_gate, gla_norm_g, conv_w, conv_norm_g, w_out, ln1_g, ln1_b,
           peer_wq, peer_keys, peer_u, peer_v, ln2_g, ln2_b):
    bsz, seq, d = x.shape
    depth = ada_w.shape[0]
    alpha = (2.0 * depth) ** 0.25
    t = bsz * seq

    mod = _ada_mod(c, ada_w, ada_b).reshape(depth, bsz, 6, d)

    qkvr = 2 * GLA_KEY_WIDTH + 2 * GLA_WIDTH
    w_alr = jnp.pad(w_in[:, :, qkvr:qkvr + GLA_GATE_RANK], ((0, 0), (0, 0), (0, ALR_PAD - GLA_GATE_RANK)))
    w_in_p = jnp.concatenate([w_in[:, :, :qkvr], w_in[:, :, qkvr + GLA_GATE_RANK:], w_alr], axis=-1).astype(BF16)
    wg_p = jnp.pad(w_gate2, ((0, 0), (0, ALR_PAD - GLA_GATE_RANK), (0, 0)))
    w_out_b = w_out.astype(BF16)
    wq_b = peer_wq.astype(BF16)
    keys_g = peer_keys.reshape(depth, 2 * PEER_HEADS, N_KEYS, PEER_HALF).astype(BF16)
    uv = jnp.concatenate([peer_u, peer_v], axis=-1)

    x2 = x.reshape(t, d)
    for l in range(depth):
        proj = _in_proj(x2, mod[l], w_in_p[l], seq)
        y = _gla_conv(proj, wg_p[l], b_gate[l][None], gla_norm_g[l][None], conv_w[l], conv_norm_g[l][None],
                      bsz, seq)
        x2 = _out_proj_ln(y, x2, mod[l], w_out_b[l], ln1_g[l][None], ln1_b[l][None], seq, alpha)
        s = _peer_scores(x2, mod[l], wq_b[l], keys_g[l], seq)
        experts, gates = _peer_topk(s)
        x2 = _peer_mix(experts, gates, x2, mod[l], ln2_g[l][None], ln2_b[l][None], uv[l], seq, alpha)
    return x2.reshape(bsz, seq, d)
```

```python
import functools
import math

import jax
import jax.numpy as jnp
from jax import lax
from jax.experimental import pallas as pl
from jax.experimental.pallas import tpu as pltpu

CHUNK = 64
GLA_HEADS = 4
GLA_DK = 128
GLA_DV = 256
GLA_KEY_WIDTH = GLA_HEADS * GLA_DK
GLA_WIDTH = GLA_HEADS * GLA_DV
GLA_GATE_RANK = 16
GLA_GATE_TAU = 16.0
CONV_WIDTH = 1024
PEER_HEADS = 8
N_KEYS = 128
PEER_TOPK = 16
PEER_HALF = 128
EPS = 1e-5

LANES = 128
SUBLANES = 8
VMEM_LIMIT_BYTES = 56 * 1024 * 1024

ALR_PAD = LANES
PROJ_WIDTH = 2 * GLA_KEY_WIDTH + 2 * GLA_WIDTH + 3 * CONV_WIDTH + ALR_PAD

F32 = jnp.float32
BF16 = jnp.bfloat16
NEG_INF = float("-inf")


def _cparams(*sem):
    return pltpu.CompilerParams(dimension_semantics=sem, vmem_limit_bytes=VMEM_LIMIT_BYTES)


def _layer_norm(z, g, b):
    mu = jnp.mean(z, axis=-1, keepdims=True)
    zc = z - mu
    var = jnp.mean(zc * zc, axis=-1, keepdims=True)
    return zc * lax.rsqrt(var + EPS) * g + b


def _rms_norm(z, g):
    return z * lax.rsqrt(jnp.mean(z * z, axis=-1, keepdims=True) + EPS) * g


def _sigmoid(z):
    return 1.0 / (1.0 + jnp.exp(-z))


def _ada_kernel(c_ref, w_ref, b_ref, o_ref):
    c = c_ref[...]
    c_act = (c * _sigmoid(c)).astype(BF16)
    o_ref[...] = jnp.dot(c_act, w_ref[...].astype(BF16), preferred_element_type=F32) + b_ref[...]


def _ada_mod(c, ada_w, ada_b):
    depth, d, n = ada_w.shape
    bsz = c.shape[0]
    tn = 1024
    return pl.pallas_call(
        _ada_kernel,
        out_shape=jax.ShapeDtypeStruct((depth, bsz, n), F32),
        grid=(depth, n // tn),
        in_specs=[
            pl.BlockSpec((bsz, d), lambda l, j: (0, 0)),
            pl.BlockSpec((None, d, tn), lambda l, j: (l, 0, j)),
            pl.BlockSpec((None, 1, tn), lambda l, j: (l, 0, j)),
        ],
        out_specs=pl.BlockSpec((None, bsz, tn), lambda l, j: (l, 0, j)),
        compiler_params=_cparams("arbitrary", "arbitrary"),
        name="ada_mod",
    )(c, ada_w, ada_b.reshape(depth, 1, n))


def _in_proj_kernel(x_ref, mod_ref, w_ref, o_ref, h_ref):
    @pl.when(pl.program_id(1) == 0)
    def _():
        sh = mod_ref[0:1, :]
        sc = mod_ref[1:2, :]
        h_ref[...] = (x_ref[...] * (1.0 + sc) + sh).astype(BF16)

    o_ref[...] = jnp.dot(h_ref[...], w_ref[...], preferred_element_type=F32)


def _in_proj(x2, mod_l, w_in_p, seq):
    t, d = x2.shape
    n = w_in_p.shape[1]
    tm, tn = 512, 896
    per_b = seq // tm
    return pl.pallas_call(
        _in_proj_kernel,
        out_shape=jax.ShapeDtypeStruct((t, n), F32),
        grid=(t // tm, n // tn),
        in_specs=[
            pl.BlockSpec((tm, d), lambda i, j: (i, 0)),
            pl.BlockSpec((None, 6, d), lambda i, j: (i // per_b, 0, 0)),
            pl.BlockSpec((d, tn), lambda i, j: (0, j)),
        ],
        out_specs=pl.BlockSpec((tm, tn), lambda i, j: (i, j)),
        scratch_shapes=[pltpu.VMEM((tm, d), BF16)],
        compiler_params=_cparams("arbitrary", "arbitrary"),
        name="in_proj",
    )(x2, mod_l, w_in_p)


GLA_ROWS = 256


def _gla_conv_kernel(q_ref, k_ref, v_ref, r_ref, cb_ref, cc_ref, ch_ref, alr_ref,
                     wg_ref, bg_ref, gng_ref, cw_ref, cng_ref, y_ref, state_ref, carry_ref):
    @pl.when(pl.program_id(1) == 0)
    def _():
        state_ref[...] = jnp.zeros_like(state_ref)
        carry_ref[...] = jnp.zeros_like(carry_ref)

    logit = jnp.dot(alr_ref[...], wg_ref[...], preferred_element_type=F32,
                    precision=lax.Precision.HIGHEST) + bg_ref[...]
    log_a = (jnp.minimum(logit, 0.0) - jnp.log(1.0 + jnp.exp(-jnp.abs(logit)))) * (1.0 / GLA_GATE_TAU)
    row = lax.broadcasted_iota(jnp.int32, (CHUNK, CHUNK), 0)
    col = lax.broadcasted_iota(jnp.int32, (CHUNK, CHUNK), 1)
    tri = (col <= row).astype(F32)
    gng = gng_ref[...]
    scale = GLA_DK ** -0.5
    for c in range(GLA_ROWS // CHUNK):
        rows = slice(c * CHUNK, (c + 1) * CHUNK)
        cum = jnp.dot(tri, log_a[rows], preferred_element_type=F32, precision=lax.Precision.HIGHEST)
        total = cum[CHUNK - 1:CHUNK, :]
        k_dec = (k_ref[rows, :] * jnp.exp(total - cum)).astype(BF16)
        dec = jnp.exp(total)
        qs = (q_ref[rows, :] * scale).astype(BF16)
        for h in range(GLA_HEADS):
            ks = slice(h * GLA_DK, (h + 1) * GLA_DK)
            vs = slice(h * GLA_DV, (h + 1) * GLA_DV)
            vh = v_ref[rows, vs].astype(BF16)
            upd = lax.dot_general(vh, k_dec[:, ks], (((0,), (0,)), ((), ())), preferred_element_type=F32)
            st = state_ref[h] * dec[:, ks] + upd
            state_ref[h] = st
            o = lax.dot_general(qs[:, ks], st.astype(BF16), (((1,), (1,)), ((), ())),
                                preferred_element_type=F32)
            rh = r_ref[rows, vs]
            y_ref[rows, vs] = (_rms_norm(o, gng) * (rh * _sigmoid(rh))).astype(y_ref.dtype)

    u = cc_ref[...] * ch_ref[...]
    prev = carry_ref[...]
    ridx = lax.broadcasted_iota(jnp.int32, (GLA_ROWS, 1), 0)
    u1 = jnp.where(ridx == 0, prev[SUBLANES - 1:SUBLANES, :], pltpu.roll(u, 1, 0))
    u2 = jnp.where(ridx == 0, prev[SUBLANES - 2:SUBLANES - 1, :],
                   jnp.where(ridx == 1, prev[SUBLANES - 1:SUBLANES, :], pltpu.roll(u, 2, 0)))
    conv = cw_ref[0:1, :] * u2 + cw_ref[1:2, :] * u1 + cw_ref[2:3, :] * u
    carry_ref[...] = u[GLA_ROWS - SUBLANES:, :]
    y_ref[:, GLA_WIDTH:] = _rms_norm(cb_ref[...] * conv, cng_ref[...]).astype(y_ref.dtype)


def _gla_conv(proj, wg_p, bg, gng, cw, cng, bsz, seq):
    t = proj.shape[0]
    nblk = seq // GLA_ROWS
    rmap = lambda width_blocks: (lambda b, j: (b * nblk + j, width_blocks))
    kw, vw = GLA_KEY_WIDTH, GLA_WIDTH
    const = lambda b, j: (0, 0)
    return pl.pallas_call(
        _gla_conv_kernel,
        out_shape=jax.ShapeDtypeStruct((t, GLA_WIDTH + CONV_WIDTH), BF16),
        grid=(bsz, nblk),
        in_specs=[
            pl.BlockSpec((GLA_ROWS, kw), rmap(0)),
            pl.BlockSpec((GLA_ROWS, kw), rmap(1)),
            pl.BlockSpec((GLA_ROWS, vw), rmap(1)),
            pl.BlockSpec((GLA_ROWS, vw), rmap(2)),
            pl.BlockSpec((GLA_ROWS, CONV_WIDTH), rmap(3)),
            pl.BlockSpec((GLA_ROWS, CONV_WIDTH), rmap(4)),
            pl.BlockSpec((GLA_ROWS, CONV_WIDTH), rmap(5)),
            pl.BlockSpec((GLA_ROWS, ALR_PAD), rmap(6144 // ALR_PAD)),
            pl.BlockSpec((ALR_PAD, kw), const),
            pl.BlockSpec((1, kw), const),
            pl.BlockSpec((1, GLA_DV), const),
            pl.BlockSpec((3, CONV_WIDTH), const),
            pl.BlockSpec((1, CONV_WIDTH), const),
        ],
        out_specs=pl.BlockSpec((GLA_ROWS, GLA_WIDTH + CONV_WIDTH), lambda b, j: (b * nblk + j, 0)),
        scratch_shapes=[pltpu.VMEM((GLA_HEADS, GLA_DV, GLA_DK), F32),
                        pltpu.VMEM((SUBLANES, CONV_WIDTH), F32)],
        compiler_params=_cparams("arbitrary", "arbitrary"),
        name="gla_conv",
    )(proj, proj, proj, proj, proj, proj, proj, proj, wg_p, bg, gng, cw, cng)


def _out_proj_ln_kernel(alpha, y_ref, x_ref, mod_ref, w_ref, g_ref, b_ref, o_ref):
    mix = jnp.dot(y_ref[...], w_ref[...], preferred_element_type=F32)
    g1 = mod_ref[2:3, :]
    o_ref[...] = _layer_norm(alpha * x_ref[...] + (1.0 + g1) * mix, g_ref[...], b_ref[...])


def _out_proj_ln(y, x2, mod_l, w_out, ln_g, ln_b, seq, alpha):
    t, d = x2.shape
    tm = 512
    per_b = seq // tm
    const = lambda i: (0, 0)
    return pl.pallas_call(
        functools.partial(_out_proj_ln_kernel, alpha),
        out_shape=jax.ShapeDtypeStruct((t, d), F32),
        grid=(t // tm,),
        in_specs=[
            pl.BlockSpec((tm, y.shape[1]), lambda i: (i, 0)),
            pl.BlockSpec((tm, d), lambda i: (i, 0)),
            pl.BlockSpec((None, 6, d), lambda i: (i // per_b, 0, 0)),
            pl.BlockSpec(w_out.shape, const),
            pl.BlockSpec((1, d), const),
            pl.BlockSpec((1, d), const),
        ],
        out_specs=pl.BlockSpec((tm, d), lambda i: (i, 0)),
        compiler_params=_cparams("arbitrary"),
        name="out_proj_ln",
    )(y, x2, mod_l, w_out, ln_g, ln_b)


def _peer_scores_kernel(x_ref, mod_ref, wq_ref, keys_ref, s_ref):
    sh = mod_ref[3:4, :]
    sc = mod_ref[4:5, :]
    h = (x_ref[...] * (1.0 + sc) + sh).astype(BF16)
    q = jnp.dot(h, wq_ref[...], preferred_element_type=F32).astype(BF16)
    for g in range(2 * PEER_HEADS):
        qg = q[:, g * PEER_HALF:(g + 1) * PEER_HALF]
        s_ref[g] = lax.dot_general(keys_ref[g], qg, (((1,), (1,)), ((), ())), preferred_element_type=F32)


def _peer_scores(x2, mod_l, wq, keys_g, seq):
    t, d = x2.shape
    tm = 512
    per_b = seq // tm
    ng = 2 * PEER_HEADS
    return pl.pallas_call(
        _peer_scores_kernel,
        out_shape=jax.ShapeDtypeStruct((ng, N_KEYS, t), F32),
        grid=(t // tm,),
        in_specs=[
            pl.BlockSpec((tm, d), lambda i: (i, 0)),
            pl.BlockSpec((None, 6, d), lambda i: (i // per_b, 0, 0)),
            pl.BlockSpec(wq.shape, lambda i: (0, 0)),
            pl.BlockSpec(keys_g.shape, lambda i: (0, 0, 0)),
        ],
        out_specs=pl.BlockSpec((ng, N_KEYS, tm), lambda i: (0, 0, i)),
        compiler_params=_cparams("arbitrary"),
        name="peer_scores",
    )(x2, mod_l, wq, keys_g)


TOPK_TOKENS = 128


def _top16_rows(s, n_rows):
    iota = lax.broadcasted_iota(jnp.int32, s.shape, 0)
    vals, ids = [], []
    for _ in range(PEER_TOPK):
        m = jnp.max(s, axis=0, keepdims=True)
        idx = jnp.min(jnp.where(s == m, iota, n_rows), axis=0, keepdims=True)
        vals.append(m)
        ids.append(idx)
        s = jnp.where(iota == idx, NEG_INF, s)
    return jnp.concatenate(vals, axis=0), jnp.concatenate(ids, axis=0)


def _select_rows(table, sel):
    out = jnp.zeros(sel.shape, table.dtype)
    for a in range(PEER_TOPK):
        out = jnp.where(sel == a, table[a:a + 1, :], out)
    return out


def _peer_topk_kernel(s_ref, e_ref, g_ref, et_ref, gt_ref):
    def head(hd, carry):
        v1, i1 = _top16_rows(s_ref[2 * hd], N_KEYS)
        v2, i2 = _top16_rows(s_ref[2 * hd + 1], N_KEYS)
        cand = jnp.concatenate([v1[a:a + 1, :] + v2 for a in range(PEER_TOPK)], axis=0)
        cv, ci = _top16_rows(cand, PEER_TOPK * PEER_TOPK)
        e1 = _select_rows(i1, ci // PEER_TOPK)
        e2 = _select_rows(i2, ci % PEER_TOPK)
        p = jnp.exp(cv - cv[0:1, :])
        gates = p / jnp.sum(p, axis=0, keepdims=True)
        rows = pl.ds(pl.multiple_of(hd * PEER_TOPK, PEER_TOPK), PEER_TOPK)
        et_ref[rows, :] = e1 * N_KEYS + e2
        gt_ref[rows, :] = gates
        return carry

    lax.fori_loop(0, PEER_HEADS, head, 0)
    e_ref[...] = et_ref[...].T
    g_ref[...] = gt_ref[...].T


def _peer_topk(s):
    ng, nk, t = s.shape
    tt = TOPK_TOKENS
    width = PEER_HEADS * PEER_TOPK
    return pl.pallas_call(
        _peer_topk_kernel,
        out_shape=(jax.ShapeDtypeStruct((t, width), jnp.int32), jax.ShapeDtypeStruct((t, width), F32)),
        grid=(t // tt,),
        in_specs=[pl.BlockSpec((ng, nk, tt), lambda i: (0, 0, i))],
        out_specs=(pl.BlockSpec((tt, width), lambda i: (i, 0)), pl.BlockSpec((tt, width), lambda i: (i, 0))),
        scratch_shapes=[pltpu.VMEM((width, tt), jnp.int32), pltpu.VMEM((width, tt), F32)],
        compiler_params=_cparams("arbitrary"),
        name="peer_topk",
    )(s)


PEER_TOKENS = 16
PEER_PICKS = PEER_HEADS * PEER_TOPK


def _gelu_tanh(z):
    return 0.5 * z * (1.0 + jnp.tanh(math.sqrt(2.0 / math.pi) * (z + 0.044715 * (z * z * z))))


def _unpack_pair(w32):
    lo = lax.bitcast_convert_type(w32 << 16, F32)
    hi = lax.bitcast_convert_type(w32 & jnp.uint32(0xFFFF0000), F32)
    return lo, hi


def _peer_mix_kernel(alpha, e_cur_ref, e_nxt_ref, gates_ref, x_ref, mod_ref, g_ref, b_ref, uv_ref,
                     o_ref, buf_ref, h_ref, ffn_ref, sem_ref):
    i = pl.program_id(0)
    n = pl.num_programs(0)
    d = x_ref.shape[1]
    hd = d // 2
    slot = lax.rem(i, 2)
    nslot = 1 - slot

    def row_copy(e, s, t, k):
        return pltpu.make_async_copy(uv_ref.at[e], buf_ref.at[s, t, pl.ds(k, 1), :], sem_ref.at[s])

    def wait_slot(s):
        pltpu.make_async_copy(buf_ref.at[s], buf_ref.at[s], sem_ref.at[s]).wait()

    @pl.when(i == 0)
    def _():
        def tok0(t, carry):
            for k in range(PEER_PICKS):
                row_copy(e_cur_ref[t, k], 0, t, k).start()
            return carry
        lax.fori_loop(0, PEER_TOKENS, tok0, 0)

    wait_slot(slot)

    x = x_ref[...]
    h_ref[...] = x * (1.0 + mod_ref[4:5, :]) + mod_ref[3:4, :]
    lane = lax.broadcasted_iota(jnp.int32, (SUBLANES, PEER_PICKS), 1)
    sub = lax.broadcasted_iota(jnp.int32, (SUBLANES, PEER_PICKS), 0)

    def tok(t, carry):
        h_lo = h_ref[pl.ds(t, 1), 0:hd]
        h_hi = h_ref[pl.ds(t, 1), hd:d]
        grow = gates_ref[pl.ds(t, 1), :]
        acc_lo = jnp.zeros((SUBLANES, hd), F32)
        acc_hi = jnp.zeros((SUBLANES, hd), F32)
        for k in range(PEER_PICKS):
            row_copy(e_nxt_ref[t, k], nslot, t, k).start()
        for g in range(PEER_PICKS // SUBLANES):
            k0 = g * SUBLANES
            w32 = buf_ref[slot, t, pl.ds(k0, SUBLANES), :]
            u_lo, u_hi = _unpack_pair(w32[:, 0:hd])
            a = jnp.sum(u_lo * h_lo + u_hi * h_hi, axis=1, keepdims=True)
            gate = jnp.sum(jnp.where(lane == sub + k0, grow, 0.0), axis=1, keepdims=True)
            w = gate * _gelu_tanh(a)
            v_lo, v_hi = _unpack_pair(w32[:, hd:d])
            acc_lo = acc_lo + w * v_lo
            acc_hi = acc_hi + w * v_hi
        ffn_ref[pl.ds(t, 1), 0:hd] = jnp.sum(acc_lo, axis=0, keepdims=True)
        ffn_ref[pl.ds(t, 1), hd:d] = jnp.sum(acc_hi, axis=0, keepdims=True)
        return carry

    lax.fori_loop(0, PEER_TOKENS, tok, 0)

    @pl.when(i == n - 1)
    def _():
        wait_slot(nslot)

    g2 = mod_ref[5:6, :]
    o_ref[...] = _layer_norm(alpha * x + (1.0 + g2) * ffn_ref[...], g_ref[...], b_ref[...])


def _peer_mix(experts, gates, x2, mod_l, ln_g, ln_b, uvp, seq, alpha):
    t, d = x2.shape
    tb = PEER_TOKENS
    nblk = t // tb
    per_b = seq // tb
    const = lambda i: (0, 0)
    return pl.pallas_call(
        functools.partial(_peer_mix_kernel, alpha),
        out_shape=jax.ShapeDtypeStruct((t, d), F32),
        grid=(nblk,),
        in_specs=[
            pl.BlockSpec((tb, PEER_PICKS), lambda i: (i, 0), memory_space=pltpu.SMEM),
            pl.BlockSpec((tb, PEER_PICKS), lambda i: (jnp.minimum(i + 1, nblk - 1), 0),
                         memory_space=pltpu.SMEM),
            pl.BlockSpec((tb, PEER_PICKS), lambda i: (i, 0)),
            pl.BlockSpec((tb, d), lambda i: (i, 0)),
            pl.BlockSpec((None, 6, d), lambda i: (i // per_b, 0, 0)),
            pl.BlockSpec((1, d), const),
            pl.BlockSpec((1, d), const),
            pl.BlockSpec(memory_space=pl.ANY),
        ],
        out_specs=pl.BlockSpec((tb, d), lambda i: (i, 0)),
        scratch_shapes=[pltpu.VMEM((2, tb, PEER_PICKS, d), jnp.uint32),
                        pltpu.VMEM((tb, d), F32),
                        pltpu.VMEM((tb, d), F32),
                        pltpu.SemaphoreType.DMA((2,))],
        compiler_params=_cparams("arbitrary"),
        name="peer_mix",
    )(experts, experts, gates, x2, mod_l, ln_g, ln_b, uvp)


def _pack_bf16_pairs(a):
    half = a.shape[-1] // 2
    bits = lax.bitcast_convert_type(a.astype(jnp.bfloat16), jnp.uint16).astype(jnp.uint32)
    return bits[..., :half] | (bits[..., half:] << 16)


def kernel(x, c, ada_w, ada_b, w_in, w_gate2, b_gate, gla_norm_g, conv_w, conv_norm_g, w_out, ln1_g, ln1_b,
           peer_wq, peer_keys, peer_u, peer_v, ln2_g, ln2_b):
    bsz, seq, d = x.shape
    depth = ada_w.shape[0]
    alpha = (2.0 * depth) ** 0.25
    t = bsz * seq

    mod = _ada_mod(c, ada_w, ada_b).reshape(depth, bsz, 6, d)

    qkvr = 2 * GLA_KEY_WIDTH + 2 * GLA_WIDTH
    w_alr = jnp.pad(w_in[:, :, qkvr:qkvr + GLA_GATE_RANK], ((0, 0), (0, 0), (0, ALR_PAD - GLA_GATE_RANK)))
    w_in_p = jnp.concatenate([w_in[:, :, :qkvr], w_in[:, :, qkvr + GLA_GATE_RANK:], w_alr], axis=-1).astype(BF16)
    wg_p = jnp.pad(w_gate2, ((0, 0), (0, ALR_PAD - GLA_GATE_RANK), (0, 0)))
    w_out_b = w_out.astype(BF16)
    wq_b = peer_wq.astype(BF16)
    keys_g = peer_keys.reshape(depth, 2 * PEER_HEADS, N_KEYS, PEER_HALF).astype(BF16)
    uv = jnp.concatenate([_pack_bf16_pairs(peer_u), _pack_bf16_pairs(peer_v)], axis=-1)[:, :, None, :]

    x2 = x.reshape(t, d)
    for l in range(depth):
        proj = _in_proj(x2, mod[l], w_in_p[l], seq)
        y = _gla_conv(proj, wg_p[l], b_gate[l][None], gla_norm_g[l][None], conv_w[l], conv_norm_g[l][None],
                      bsz, seq)
        x2 = _out_proj_ln(y, x2, mod[l], w_out_b[l], ln1_g[l][None], ln1_b[l][None], seq, alpha)
        s = _peer_scores(x2, mod[l], wq_b[l], keys_g[l], seq)
        experts, gates = _peer_topk(s)
        x2 = _peer_mix(experts, gates, x2, mod[l], ln2_g[l][None], ln2_b[l][None], uv[l], seq, alpha)
    return x2.reshape(bsz, seq, d)
```

```python
import functools
import math

import jax
import jax.numpy as jnp
from jax import lax
from jax.experimental import pallas as pl
from jax.experimental.pallas import tpu as pltpu

CHUNK = 64
GLA_HEADS = 4
GLA_DK = 128
GLA_DV = 256
GLA_KEY_WIDTH = GLA_HEADS * GLA_DK
GLA_WIDTH = GLA_HEADS * GLA_DV
GLA_GATE_RANK = 16
GLA_GATE_TAU = 16.0
CONV_WIDTH = 1024
PEER_HEADS = 8
N_KEYS = 128
PEER_TOPK = 16
PEER_HALF = 128
EPS = 1e-5

LANES = 128
SUBLANES = 8
VMEM_LIMIT_BYTES = 56 * 1024 * 1024

ALR_PAD = LANES
PROJ_WIDTH = 2 * GLA_KEY_WIDTH + 2 * GLA_WIDTH + 3 * CONV_WIDTH + ALR_PAD

F32 = jnp.float32
BF16 = jnp.bfloat16
NEG_INF = float("-inf")


def _cparams(*sem):
    return pltpu.CompilerParams(dimension_semantics=sem, vmem_limit_bytes=VMEM_LIMIT_BYTES)


def _layer_norm(z, g, b):
    mu = jnp.mean(z, axis=-1, keepdims=True)
    zc = z - mu
    var = jnp.mean(zc * zc, axis=-1, keepdims=True)
    return zc * lax.rsqrt(var + EPS) * g + b


def _rms_norm(z, g):
    return z * lax.rsqrt(jnp.mean(z * z, axis=-1, keepdims=True) + EPS) * g


def _sigmoid(z):
    return 1.0 / (1.0 + jnp.exp(-z))


def _ada_kernel(c_ref, w_ref, b_ref, o_ref):
    c = c_ref[...]
    c_act = (c * _sigmoid(c)).astype(BF16)
    o_ref[...] = jnp.dot(c_act, w_ref[...].astype(BF16), preferred_element_type=F32) + b_ref[...]


def _ada_mod(c, ada_w, ada_b):
    depth, d, n = ada_w.shape
    bsz = c.shape[0]
    tn = 1024
    return pl.pallas_call(
        _ada_kernel,
        out_shape=jax.ShapeDtypeStruct((depth, bsz, n), F32),
        grid=(depth, n // tn),
        in_specs=[
            pl.BlockSpec((bsz, d), lambda l, j: (0, 0)),
            pl.BlockSpec((None, d, tn), lambda l, j: (l, 0, j)),
            pl.BlockSpec((None, 1, tn), lambda l, j: (l, 0, j)),
        ],
        out_specs=pl.BlockSpec((None, bsz, tn), lambda l, j: (l, 0, j)),
        compiler_params=_cparams("arbitrary", "arbitrary"),
        name="ada_mod",
    )(c, ada_w, ada_b.reshape(depth, 1, n))


def _in_proj_kernel(x_ref, mod_ref, w_ref, o_ref, h_ref):
    @pl.when(pl.program_id(1) == 0)
    def _():
        sh = mod_ref[0:1, :]
        sc = mod_ref[1:2, :]
        h_ref[...] = (x_ref[...] * (1.0 + sc) + sh).astype(BF16)

    o_ref[...] = jnp.dot(h_ref[...], w_ref[...], preferred_element_type=F32)


def _in_proj(x2, mod_l, w_in_p, seq):
    t, d = x2.shape
    n = w_in_p.shape[1]
    tm, tn = 512, 896
    per_b = seq // tm
    return pl.pallas_call(
        _in_proj_kernel,
        out_shape=jax.ShapeDtypeStruct((t, n), F32),
        grid=(t // tm, n // tn),
        in_specs=[
            pl.BlockSpec((tm, d), lambda i, j: (i, 0)),
            pl.BlockSpec((None, 6, d), lambda i, j: (i // per_b, 0, 0)),
            pl.BlockSpec((d, tn), lambda i, j: (0, j)),
        ],
        out_specs=pl.BlockSpec((tm, tn), lambda i, j: (i, j)),
        scratch_shapes=[pltpu.VMEM((tm, d), BF16)],
        compiler_params=_cparams("arbitrary", "arbitrary"),
        name="in_proj",
    )(x2, mod_l, w_in_p)


GLA_ROWS = 256


def _gla_conv_kernel(q_ref, k_ref, v_ref, r_ref, cb_ref, cc_ref, ch_ref, alr_ref,
                     wg_ref, bg_ref, gng_ref, cw_ref, cng_ref, y_ref, state_ref, carry_ref):
    @pl.when(pl.program_id(1) == 0)
    def _():
        state_ref[...] = jnp.zeros_like(state_ref)
        carry_ref[...] = jnp.zeros_like(carry_ref)

    logit = jnp.dot(alr_ref[...], wg_ref[...], preferred_element_type=F32,
                    precision=lax.Precision.HIGHEST) + bg_ref[...]
    log_a = (jnp.minimum(logit, 0.0) - jnp.log(1.0 + jnp.exp(-jnp.abs(logit)))) * (1.0 / GLA_GATE_TAU)
    row = lax.broadcasted_iota(jnp.int32, (CHUNK, CHUNK), 0)
    col = lax.broadcasted_iota(jnp.int32, (CHUNK, CHUNK), 1)
    tri = (col <= row).astype(F32)
    gng = gng_ref[...]
    scale = GLA_DK ** -0.5
    for c in range(GLA_ROWS // CHUNK):
        rows = slice(c * CHUNK, (c + 1) * CHUNK)
        cum = jnp.dot(tri, log_a[rows], preferred_element_type=F32, precision=lax.Precision.HIGHEST)
        total = cum[CHUNK - 1:CHUNK, :]
        k_dec = (k_ref[rows, :] * jnp.exp(total - cum)).astype(BF16)
        dec = jnp.exp(total)
        qs = (q_ref[rows, :] * scale).astype(BF16)
        for h in range(GLA_HEADS):
            ks = slice(h * GLA_DK, (h + 1) * GLA_DK)
            vs = slice(h * GLA_DV, (h + 1) * GLA_DV)
            vh = v_ref[rows, vs].astype(BF16)
            upd = lax.dot_general(vh, k_dec[:, ks], (((0,), (0,)), ((), ())), preferred_element_type=F32)
            st = state_ref[h] * dec[:, ks] + upd
            state_ref[h] = st
            o = lax.dot_general(qs[:, ks], st.astype(BF16), (((1,), (1,)), ((), ())),
                                preferred_element_type=F32)
            rh = r_ref[rows, vs]
            y_ref[rows, vs] = (_rms_norm(o, gng) * (rh * _sigmoid(rh))).astype(y_ref.dtype)

    u = cc_ref[...] * ch_ref[...]
    prev = carry_ref[...]
    ridx = lax.broadcasted_iota(jnp.int32, (GLA_ROWS, 1), 0)
    u1 = jnp.where(ridx == 0, prev[SUBLANES - 1:SUBLANES, :], pltpu.roll(u, 1, 0))
    u2 = jnp.where(ridx == 0, prev[SUBLANES - 2:SUBLANES - 1, :],
                   jnp.where(ridx == 1, prev[SUBLANES - 1:SUBLANES, :], pltpu.roll(u, 2, 0)))
    conv = cw_ref[0:1, :] * u2 + cw_ref[1:2, :] * u1 + cw_ref[2:3, :] * u
    carry_ref[...] = u[GLA_ROWS - SUBLANES:, :]
    y_ref[:, GLA_WIDTH:] = _rms_norm(cb_ref[...] * conv, cng_ref[...]).astype(y_ref.dtype)


def _gla_conv(proj, wg_p, bg, gng, cw, cng, bsz, seq):
    t = proj.shape[0]
    nblk = seq // GLA_ROWS
    rmap = lambda width_blocks: (lambda b, j: (b * nblk + j, width_blocks))
    kw, vw = GLA_KEY_WIDTH, GLA_WIDTH
    const = lambda b, j: (0, 0)
    return pl.pallas_call(
        _gla_conv_kernel,
        out_shape=jax.ShapeDtypeStruct((t, GLA_WIDTH + CONV_WIDTH), BF16),
        grid=(bsz, nblk),
        in_specs=[
            pl.BlockSpec((GLA_ROWS, kw), rmap(0)),
            pl.BlockSpec((GLA_ROWS, kw), rmap(1)),
            pl.BlockSpec((GLA_ROWS, vw), rmap(1)),
            pl.BlockSpec((GLA_ROWS, vw), rmap(2)),
            pl.BlockSpec((GLA_ROWS, CONV_WIDTH), rmap(3)),
            pl.BlockSpec((GLA_ROWS, CONV_WIDTH), rmap(4)),
            pl.BlockSpec((GLA_ROWS, CONV_WIDTH), rmap(5)),
            pl.BlockSpec((GLA_ROWS, ALR_PAD), rmap(6144 // ALR_PAD)),
            pl.BlockSpec((ALR_PAD, kw), const),
            pl.BlockSpec((1, kw), const),
            pl.BlockSpec((1, GLA_DV), const),
            pl.BlockSpec((3, CONV_WIDTH), const),
            pl.BlockSpec((1, CONV_WIDTH), const),
        ],
        out_specs=pl.BlockSpec((GLA_ROWS, GLA_WIDTH + CONV_WIDTH), lambda b, j: (b * nblk + j, 0)),
        scratch_shapes=[pltpu.VMEM((GLA_HEADS, GLA_DV, GLA_DK), F32),
                        pltpu.VMEM((SUBLANES, CONV_WIDTH), F32)],
        compiler_params=_cparams("arbitrary", "arbitrary"),
        name="gla_conv",
    )(proj, proj, proj, proj, proj, proj, proj, proj, wg_p, bg, gng, cw, cng)


def _out_proj_ln_kernel(alpha, y_ref, x_ref, mod_ref, w_ref, g_ref, b_ref, o_ref):
    mix = jnp.dot(y_ref[...], w_ref[...], preferred_element_type=F32)
    g1 = mod_ref[2:3, :]
    o_ref[...] = _layer_norm(alpha * x_ref[...] + (1.0 + g1) * mix, g_ref[...], b_ref[...])


def _out_proj_ln(y, x2, mod_l, w_out, ln_g, ln_b, seq, alpha):
    t, d = x2.shape
    tm = 512
    per_b = seq // tm
    const = lambda i: (0, 0)
    return pl.pallas_call(
        functools.partial(_out_proj_ln_kernel, alpha),
        out_shape=jax.ShapeDtypeStruct((t, d), F32),
        grid=(t // tm,),
        in_specs=[
            pl.BlockSpec((tm, y.shape[1]), lambda i: (i, 0)),
            pl.BlockSpec((tm, d), lambda i: (i, 0)),
            pl.BlockSpec((None, 6, d), lambda i: (i // per_b, 0, 0)),
            pl.BlockSpec(w_out.shape, const),
            pl.BlockSpec((1, d), const),
            pl.BlockSpec((1, d), const),
        ],
        out_specs=pl.BlockSpec((tm, d), lambda i: (i, 0)),
        compiler_params=_cparams("arbitrary"),
        name="out_proj_ln",
    )(y, x2, mod_l, w_out, ln_g, ln_b)


def _peer_scores_kernel(x_ref, mod_ref, wq_ref, keys_ref, s_ref):
    sh = mod_ref[3:4, :]
    sc = mod_ref[4:5, :]
    h = (x_ref[...] * (1.0 + sc) + sh).astype(BF16)
    q = jnp.dot(h, wq_ref[...], preferred_element_type=F32).astype(BF16)
    for g in range(2 * PEER_HEADS):
        qg = q[:, g * PEER_HALF:(g + 1) * PEER_HALF]
        s_ref[g] = lax.dot_general(keys_ref[g], qg, (((1,), (1,)), ((), ())), preferred_element_type=F32)


def _peer_scores(x2, mod_l, wq, keys_g, seq):
    t, d = x2.shape
    tm = 512
    per_b = seq // tm
    ng = 2 * PEER_HEADS
    return pl.pallas_call(
        _peer_scores_kernel,
        out_shape=jax.ShapeDtypeStruct((ng, N_KEYS, t), F32),
        grid=(t // tm,),
        in_specs=[
            pl.BlockSpec((tm, d), lambda i: (i, 0)),
            pl.BlockSpec((None, 6, d), lambda i: (i // per_b, 0, 0)),
            pl.BlockSpec(wq.shape, lambda i: (0, 0)),
            pl.BlockSpec(keys_g.shape, lambda i: (0, 0, 0)),
        ],
        out_specs=pl.BlockSpec((ng, N_KEYS, tm), lambda i: (0, 0, i)),
        compiler_params=_cparams("arbitrary"),
        name="peer_scores",
    )(x2, mod_l, wq, keys_g)


TOPK_TOKENS = 128


def _top16_rows(s, n_rows):
    iota = lax.broadcasted_iota(jnp.int32, s.shape, 0)
    vals, ids = [], []
    for _ in range(PEER_TOPK):
        m = jnp.max(s, axis=0, keepdims=True)
        idx = jnp.min(jnp.where(s == m, iota, n_rows), axis=0, keepdims=True)
        vals.append(m)
        ids.append(idx)
        s = jnp.where(iota == idx, NEG_INF, s)
    return jnp.concatenate(vals, axis=0), jnp.concatenate(ids, axis=0)


def _select_rows(table, sel):
    out = jnp.zeros(sel.shape, table.dtype)
    for a in range(PEER_TOPK):
        out = jnp.where(sel == a, table[a:a + 1, :], out)
    return out


def _peer_topk_kernel(s_ref, e_ref, g_ref, et_ref, gt_ref):
    def head(hd, carry):
        v1, i1 = _top16_rows(s_ref[2 * hd], N_KEYS)
        v2, i2 = _top16_rows(s_ref[2 * hd + 1], N_KEYS)
        cand = jnp.concatenate([v1[a:a + 1, :] + v2 for a in range(PEER_TOPK)], axis=0)
        cv, ci = _top16_rows(cand, PEER_TOPK * PEER_TOPK)
        e1 = _select_rows(i1, ci // PEER_TOPK)
        e2 = _select_rows(i2, ci % PEER_TOPK)
        p = jnp.exp(cv - cv[0:1, :])
        gates = p / jnp.sum(p, axis=0, keepdims=True)
        rows = pl.ds(pl.multiple_of(hd * PEER_TOPK, PEER_TOPK), PEER_TOPK)
        et_ref[rows, :] = e1 * N_KEYS + e2
        gt_ref[rows, :] = gates
        return carry

    lax.fori_loop(0, PEER_HEADS, head, 0)
    e_ref[...] = et_ref[...].T
    g_ref[...] = gt_ref[...].T


def _peer_topk(s):
    ng, nk, t = s.shape
    tt = TOPK_TOKENS
    width = PEER_HEADS * PEER_TOPK
    return pl.pallas_call(
        _peer_topk_kernel,
        out_shape=(jax.ShapeDtypeStruct((t, width), jnp.int32), jax.ShapeDtypeStruct((t, width), F32)),
        grid=(t // tt,),
        in_specs=[pl.BlockSpec((ng, nk, tt), lambda i: (0, 0, i))],
        out_specs=(pl.BlockSpec((tt, width), lambda i: (i, 0)), pl.BlockSpec((tt, width), lambda i: (i, 0))),
        scratch_shapes=[pltpu.VMEM((width, tt), jnp.int32), pltpu.VMEM((width, tt), F32)],
        compiler_params=_cparams("arbitrary"),
        name="peer_topk",
    )(s)


PEER_PHASES = 4
PEER_PHASE_TOKENS = 4
PEER_TOKENS = PEER_PHASES * PEER_PHASE_TOKENS
PEER_PICKS = PEER_HEADS * PEER_TOPK


def _gelu_tanh(z):
    return 0.5 * z * (1.0 + jnp.tanh(math.sqrt(2.0 / math.pi) * (z + 0.044715 * (z * z * z))))


def _unpack_pair(w32):
    lo = lax.bitcast_convert_type(w32 << 16, F32)
    hi = lax.bitcast_convert_type(w32 & jnp.uint32(0xFFFF0000), F32)
    return lo, hi


def _peer_mix_kernel(alpha, e_cur_ref, e_nxt_ref, gates_ref, x_ref, mod_ref, g_ref, b_ref, uv_ref,
                     o_ref, buf0, buf1, buf2, buf3, h_ref, ffn_ref, sem_ref):
    i = pl.program_id(0)
    n = pl.num_programs(0)
    d = x_ref.shape[1]
    hd = d // 2
    bufs = (buf0, buf1, buf2, buf3)
    ptok = PEER_PHASE_TOKENS

    def start_row(e_ref, row, buf, sem, t, k):
        pltpu.make_async_copy(uv_ref.at[e_ref[row, k]], buf.at[t, pl.ds(k, 1), :], sem).start(priority=k % 2)

    def wait_buf(p):
        pltpu.make_async_copy(bufs[p], bufs[p], sem_ref.at[p]).wait()

    @pl.when(i == 0)
    def _():
        for p in range(2):
            for t in range(ptok):
                for k in range(PEER_PICKS):
                    start_row(e_cur_ref, p * ptok + t, bufs[p], sem_ref.at[p], t, k)

    x = x_ref[...]
    h_ref[...] = x * (1.0 + mod_ref[4:5, :]) + mod_ref[3:4, :]
    lane = lax.broadcasted_iota(jnp.int32, (SUBLANES, PEER_PICKS), 1)
    sub = lax.broadcasted_iota(jnp.int32, (SUBLANES, PEER_PICKS), 0)

    for p in range(PEER_PHASES):
        q = (p + 2) % PEER_PHASES
        ahead_ref, ahead_row0 = (e_cur_ref, (p + 2) * ptok) if p + 2 < PEER_PHASES else (e_nxt_ref, q * ptok)
        wait_buf(p)
        for t in range(ptok):
            r = p * ptok + t
            h_lo = h_ref[r:r + 1, 0:hd]
            h_hi = h_ref[r:r + 1, hd:d]
            grow = gates_ref[r:r + 1, :]
            acc_lo = jnp.zeros((SUBLANES, hd), F32)
            acc_hi = jnp.zeros((SUBLANES, hd), F32)
            for g in range(PEER_PICKS // SUBLANES):
                k0 = g * SUBLANES
                w32 = bufs[p][t, pl.ds(k0, SUBLANES), :]
                for j in range(SUBLANES):
                    start_row(ahead_ref, ahead_row0 + t, bufs[q], sem_ref.at[q], t, k0 + j)
                u_lo, u_hi = _unpack_pair(w32[:, 0:hd])
                a = jnp.sum(u_lo * h_lo + u_hi * h_hi, axis=1, keepdims=True)
                gate = jnp.sum(jnp.where(lane == sub + k0, grow, 0.0), axis=1, keepdims=True)
                w = gate * _gelu_tanh(a)
                v_lo, v_hi = _unpack_pair(w32[:, hd:d])
                acc_lo = acc_lo + w * v_lo
                acc_hi = acc_hi + w * v_hi
            ffn_ref[r:r + 1, 0:hd] = jnp.sum(acc_lo, axis=0, keepdims=True)
            ffn_ref[r:r + 1, hd:d] = jnp.sum(acc_hi, axis=0, keepdims=True)

    @pl.when(i == n - 1)
    def _():
        wait_buf(0)
        wait_buf(1)

    g2 = mod_ref[5:6, :]
    o_ref[...] = _layer_norm(alpha * x + (1.0 + g2) * ffn_ref[...], g_ref[...], b_ref[...])


def _peer_mix(experts, gates, x2, mod_l, ln_g, ln_b, uvp, seq, alpha):
    t, d = x2.shape
    tb = PEER_TOKENS
    nblk = t // tb
    per_b = seq // tb
    const = lambda i: (0, 0)
    return pl.pallas_call(
        functools.partial(_peer_mix_kernel, alpha),
        out_shape=jax.ShapeDtypeStruct((t, d), F32),
        grid=(nblk,),
        in_specs=[
            pl.BlockSpec((tb, PEER_PICKS), lambda i: (i, 0), memory_space=pltpu.SMEM),
            pl.BlockSpec((tb, PEER_PICKS), lambda i: (jnp.minimum(i + 1, nblk - 1), 0),
                         memory_space=pltpu.SMEM),
            pl.BlockSpec((tb, PEER_PICKS), lambda i: (i, 0)),
            pl.BlockSpec((tb, d), lambda i: (i, 0)),
            pl.BlockSpec((None, 6, d), lambda i: (i // per_b, 0, 0)),
            pl.BlockSpec((1, d), const),
            pl.BlockSpec((1, d), const),
            pl.BlockSpec(memory_space=pl.ANY),
        ],
        out_specs=pl.BlockSpec((tb, d), lambda i: (i, 0)),
        scratch_shapes=[pltpu.VMEM((PEER_PHASE_TOKENS, PEER_PICKS, d), jnp.uint32)] * PEER_PHASES + [
                        pltpu.VMEM((tb, d), F32),
                        pltpu.VMEM((tb, d), F32),
                        pltpu.SemaphoreType.DMA((PEER_PHASES,))],
        compiler_params=_cparams("arbitrary"),
        name="peer_mix",
    )(experts, experts, gates, x2, mod_l, ln_g, ln_b, uvp)


def _pack_bf16_pairs(lo, hi):
    bits = lambda a: lax.bitcast_convert_type(a.astype(jnp.bfloat16), jnp.uint16).astype(jnp.uint32)
    return bits(lo) | (bits(hi) << 16)


def kernel(x, c, ada_w, ada_b, w_in, w_gate2, b_gate, gla_norm_g, conv_w, conv_norm_g, w_out, ln1_g, ln1_b,
           peer_wq, peer_keys, peer_u, peer_v, ln2_g, ln2_b):
    bsz, seq, d = x.shape
    depth = ada_w.shape[0]
    alpha = (2.0 * depth) ** 0.25
    t = bsz * seq

    mod = _ada_mod(c, ada_w, ada_b).reshape(depth, bsz, 6, d)

    qkvr = 2 * GLA_KEY_WIDTH + 2 * GLA_WIDTH
    w_alr = jnp.pad(w_in[:, :, qkvr:qkvr + GLA_GATE_RANK], ((0, 0), (0, 0), (0, ALR_PAD - GLA_GATE_RANK)))
    w_in_p = jnp.concatenate([w_in[:, :, :qkvr], w_in[:, :, qkvr + GLA_GATE_RANK:], w_alr], axis=-1).astype(BF16)
    wg_p = jnp.pad(w_gate2, ((0, 0), (0, ALR_PAD - GLA_GATE_RANK), (0, 0)))
    w_out_b = w_out.astype(BF16)
    wq_b = peer_wq.astype(BF16)
    keys_g = peer_keys.reshape(depth, 2 * PEER_HEADS, N_KEYS, PEER_HALF).astype(BF16)
    hd = d // 2
    uv = jnp.concatenate([_pack_bf16_pairs(peer_u[..., :hd], peer_u[..., hd:]),
                          _pack_bf16_pairs(peer_v[..., :hd], peer_v[..., hd:])], axis=-1)[:, :, None, :]

    x2 = x.reshape(t, d)
    for l in range(depth):
        proj = _in_proj(x2, mod[l], w_in_p[l], seq)
        y = _gla_conv(proj, wg_p[l], b_gate[l][None], gla_norm_g[l][None], conv_w[l], conv_norm_g[l][None],
                      bsz, seq)
        x2 = _out_proj_ln(y, x2, mod[l], w_out_b[l], ln1_g[l][None], ln1_b[l][None], seq, alpha)
        s = _peer_scores(x2, mod[l], wq_b[l], keys_g[l], seq)
        experts, gates = _peer_topk(s)
        x2 = _peer_mix(experts, gates, x2, mod[l], ln2_g[l][None], ln2_b[l][None], uv[l], seq, alpha)
    return x2.reshape(bsz, seq, d)
```

```python
import functools
import math

import jax
import jax.numpy as jnp
from jax import lax
from jax.experimental import pallas as pl
from jax.experimental.pallas import tpu as pltpu

CHUNK = 64
GLA_HEADS = 4
GLA_DK = 128
GLA_DV = 256
GLA_KEY_WIDTH = GLA_HEADS * GLA_DK
GLA_WIDTH = GLA_HEADS * GLA_DV
GLA_GATE_RANK = 16
GLA_GATE_TAU = 16.0
CONV_WIDTH = 1024
PEER_HEADS = 8
N_KEYS = 128
PEER_TOPK = 16
PEER_HALF = 128
EPS = 1e-5

LANES = 128
SUBLANES = 8
VMEM_LIMIT_BYTES = 56 * 1024 * 1024

ALR_PAD = LANES
PROJ_WIDTH = 2 * GLA_KEY_WIDTH + 2 * GLA_WIDTH + 3 * CONV_WIDTH + ALR_PAD

F32 = jnp.float32
BF16 = jnp.bfloat16
NEG_INF = float("-inf")


def _cparams(*sem):
    return pltpu.CompilerParams(dimension_semantics=sem, vmem_limit_bytes=VMEM_LIMIT_BYTES)


def _layer_norm(z, g, b):
    mu = jnp.mean(z, axis=-1, keepdims=True)
    zc = z - mu
    var = jnp.mean(zc * zc, axis=-1, keepdims=True)
    return zc * lax.rsqrt(var + EPS) * g + b


def _rms_norm(z, g):
    return z * lax.rsqrt(jnp.mean(z * z, axis=-1, keepdims=True) + EPS) * g


def _sigmoid(z):
    return 1.0 / (1.0 + jnp.exp(-z))


def _ada_kernel(c_ref, w_ref, b_ref, o_ref):
    c = c_ref[...]
    c_act = (c * _sigmoid(c)).astype(BF16)
    o_ref[...] = jnp.dot(c_act, w_ref[...].astype(BF16), preferred_element_type=F32) + b_ref[...]


def _ada_mod(c, ada_w, ada_b):
    depth, d, n = ada_w.shape
    bsz = c.shape[0]
    tn = 1024
    return pl.pallas_call(
        _ada_kernel,
        out_shape=jax.ShapeDtypeStruct((depth, bsz, n), F32),
        grid=(depth, n // tn),
        in_specs=[
            pl.BlockSpec((bsz, d), lambda l, j: (0, 0)),
            pl.BlockSpec((None, d, tn), lambda l, j: (l, 0, j)),
            pl.BlockSpec((None, 1, tn), lambda l, j: (l, 0, j)),
        ],
        out_specs=pl.BlockSpec((None, bsz, tn), lambda l, j: (l, 0, j)),
        compiler_params=_cparams("arbitrary", "arbitrary"),
        name="ada_mod",
    )(c, ada_w, ada_b.reshape(depth, 1, n))


def _in_proj_kernel(x_ref, mod_ref, w_ref, o_ref, h_ref):
    @pl.when(pl.program_id(1) == 0)
    def _():
        sh = mod_ref[0:1, :]
        sc = mod_ref[1:2, :]
        h_ref[...] = (x_ref[...] * (1.0 + sc) + sh).astype(BF16)

    o_ref[...] = jnp.dot(h_ref[...], w_ref[...], preferred_element_type=F32)


def _in_proj(x2, mod_l, w_in_p, seq):
    t, d = x2.shape
    n = w_in_p.shape[1]
    tm, tn = 512, 896
    per_b = seq // tm
    return pl.pallas_call(
        _in_proj_kernel,
        out_shape=jax.ShapeDtypeStruct((t, n), F32),
        grid=(t // tm, n // tn),
        in_specs=[
            pl.BlockSpec((tm, d), lambda i, j: (i, 0)),
            pl.BlockSpec((None, 6, d), lambda i, j: (i // per_b, 0, 0)),
            pl.BlockSpec((d, tn), lambda i, j: (0, j)),
        ],
        out_specs=pl.BlockSpec((tm, tn), lambda i, j: (i, j)),
        scratch_shapes=[pltpu.VMEM((tm, d), BF16)],
        compiler_params=_cparams("arbitrary", "arbitrary"),
        name="in_proj",
    )(x2, mod_l, w_in_p)


GLA_ROWS = 256


def _gla_conv_kernel(q_ref, k_ref, v_ref, r_ref, cb_ref, cc_ref, ch_ref, alr_ref,
                     wg_ref, bg_ref, gng_ref, cw_ref, cng_ref, y_ref, state_ref, carry_ref):
    @pl.when(pl.program_id(1) == 0)
    def _():
        state_ref[...] = jnp.zeros_like(state_ref)
        carry_ref[...] = jnp.zeros_like(carry_ref)

    logit = jnp.dot(alr_ref[...], wg_ref[...], preferred_element_type=F32,
                    precision=lax.Precision.HIGHEST) + bg_ref[...]
    log_a = (jnp.minimum(logit, 0.0) - jnp.log(1.0 + jnp.exp(-jnp.abs(logit)))) * (1.0 / GLA_GATE_TAU)
    row = lax.broadcasted_iota(jnp.int32, (CHUNK, CHUNK), 0)
    col = lax.broadcasted_iota(jnp.int32, (CHUNK, CHUNK), 1)
    tri = (col <= row).astype(F32)
    gng = gng_ref[...]
    scale = GLA_DK ** -0.5
    for c in range(GLA_ROWS // CHUNK):
        rows = slice(c * CHUNK, (c + 1) * CHUNK)
        cum = jnp.dot(tri, log_a[rows], preferred_element_type=F32, precision=lax.Precision.HIGHEST)
        total = cum[CHUNK - 1:CHUNK, :]
        k_dec = (k_ref[rows, :] * jnp.exp(total - cum)).astype(BF16)
        dec = jnp.exp(total)
        qs = (q_ref[rows, :] * scale).astype(BF16)
        for h in range(GLA_HEADS):
            ks = slice(h * GLA_DK, (h + 1) * GLA_DK)
            vs = slice(h * GLA_DV, (h + 1) * GLA_DV)
            vh = v_ref[rows, vs].astype(BF16)
            upd = lax.dot_general(vh, k_dec[:, ks], (((0,), (0,)), ((), ())), preferred_element_type=F32)
            st = state_ref[h] * dec[:, ks] + upd
            state_ref[h] = st
            o = lax.dot_general(qs[:, ks], st.astype(BF16), (((1,), (1,)), ((), ())),
                                preferred_element_type=F32)
            rh = r_ref[rows, vs]
            y_ref[rows, vs] = (_rms_norm(o, gng) * (rh * _sigmoid(rh))).astype(y_ref.dtype)

    u = cc_ref[...] * ch_ref[...]
    prev = carry_ref[...]
    ridx = lax.broadcasted_iota(jnp.int32, (GLA_ROWS, 1), 0)
    u1 = jnp.where(ridx == 0, prev[SUBLANES - 1:SUBLANES, :], pltpu.roll(u, 1, 0))
    u2 = jnp.where(ridx == 0, prev[SUBLANES - 2:SUBLANES - 1, :],
                   jnp.where(ridx == 1, prev[SUBLANES - 1:SUBLANES, :], pltpu.roll(u, 2, 0)))
    conv = cw_ref[0:1, :] * u2 + cw_ref[1:2, :] * u1 + cw_ref[2:3, :] * u
    carry_ref[...] = u[GLA_ROWS - SUBLANES:, :]
    y_ref[:, GLA_WIDTH:] = _rms_norm(cb_ref[...] * conv, cng_ref[...]).astype(y_ref.dtype)


def _gla_conv(proj, wg_p, bg, gng, cw, cng, bsz, seq):
    t = proj.shape[0]
    nblk = seq // GLA_ROWS
    rmap = lambda width_blocks: (lambda b, j: (b * nblk + j, width_blocks))
    kw, vw = GLA_KEY_WIDTH, GLA_WIDTH
    const = lambda b, j: (0, 0)
    return pl.pallas_call(
        _gla_conv_kernel,
        out_shape=jax.ShapeDtypeStruct((t, GLA_WIDTH + CONV_WIDTH), BF16),
        grid=(bsz, nblk),
        in_specs=[
            pl.BlockSpec((GLA_ROWS, kw), rmap(0)),
            pl.BlockSpec((GLA_ROWS, kw), rmap(1)),
            pl.BlockSpec((GLA_ROWS, vw), rmap(1)),
            pl.BlockSpec((GLA_ROWS, vw), rmap(2)),
            pl.BlockSpec((GLA_ROWS, CONV_WIDTH), rmap(3)),
            pl.BlockSpec((GLA_ROWS, CONV_WIDTH), rmap(4)),
            pl.BlockSpec((GLA_ROWS, CONV_WIDTH), rmap(5)),
            pl.BlockSpec((GLA_ROWS, ALR_PAD), rmap(6144 // ALR_PAD)),
            pl.BlockSpec((ALR_PAD, kw), const),
            pl.BlockSpec((1, kw), const),
            pl.BlockSpec((1, GLA_DV), const),
            pl.BlockSpec((3, CONV_WIDTH), const),
            pl.BlockSpec((1, CONV_WIDTH), const),
        ],
        out_specs=pl.BlockSpec((GLA_ROWS, GLA_WIDTH + CONV_WIDTH), lambda b, j: (b * nblk + j, 0)),
        scratch_shapes=[pltpu.VMEM((GLA_HEADS, GLA_DV, GLA_DK), F32),
                        pltpu.VMEM((SUBLANES, CONV_WIDTH), F32)],
        compiler_params=_cparams("arbitrary", "arbitrary"),
        name="gla_conv",
    )(proj, proj, proj, proj, proj, proj, proj, proj, wg_p, bg, gng, cw, cng)


def _out_proj_ln_kernel(alpha, y_ref, x_ref, mod_ref, w_ref, g_ref, b_ref, o_ref):
    mix = jnp.dot(y_ref[...], w_ref[...], preferred_element_type=F32)
    g1 = mod_ref[2:3, :]
    o_ref[...] = _layer_norm(alpha * x_ref[...] + (1.0 + g1) * mix, g_ref[...], b_ref[...])


def _out_proj_ln(y, x2, mod_l, w_out, ln_g, ln_b, seq, alpha):
    t, d = x2.shape
    tm = 512
    per_b = seq // tm
    const = lambda i: (0, 0)
    return pl.pallas_call(
        functools.partial(_out_proj_ln_kernel, alpha),
        out_shape=jax.ShapeDtypeStruct((t, d), F32),
        grid=(t // tm,),
        in_specs=[
            pl.BlockSpec((tm, y.shape[1]), lambda i: (i, 0)),
            pl.BlockSpec((tm, d), lambda i: (i, 0)),
            pl.BlockSpec((None, 6, d), lambda i: (i // per_b, 0, 0)),
            pl.BlockSpec(w_out.shape, const),
            pl.BlockSpec((1, d), const),
            pl.BlockSpec((1, d), const),
        ],
        out_specs=pl.BlockSpec((tm, d), lambda i: (i, 0)),
        compiler_params=_cparams("arbitrary"),
        name="out_proj_ln",
    )(y, x2, mod_l, w_out, ln_g, ln_b)


def _peer_scores_kernel(x_ref, mod_ref, wq_ref, keys_ref, s_ref):
    sh = mod_ref[3:4, :]
    sc = mod_ref[4:5, :]
    h = (x_ref[...] * (1.0 + sc) + sh).astype(BF16)
    q = jnp.dot(h, wq_ref[...], preferred_element_type=F32).astype(BF16)
    for g in range(2 * PEER_HEADS):
        qg = q[:, g * PEER_HALF:(g + 1) * PEER_HALF]
        s_ref[g] = lax.dot_general(keys_ref[g], qg, (((1,), (1,)), ((), ())), preferred_element_type=F32)


def _peer_scores(x2, mod_l, wq, keys_g, seq):
    t, d = x2.shape
    tm = 512
    per_b = seq // tm
    ng = 2 * PEER_HEADS
    return pl.pallas_call(
        _peer_scores_kernel,
        out_shape=jax.ShapeDtypeStruct((ng, N_KEYS, t), F32),
        grid=(t // tm,),
        in_specs=[
            pl.BlockSpec((tm, d), lambda i: (i, 0)),
            pl.BlockSpec((None, 6, d), lambda i: (i // per_b, 0, 0)),
            pl.BlockSpec(wq.shape, lambda i: (0, 0)),
            pl.BlockSpec(keys_g.shape, lambda i: (0, 0, 0)),
        ],
        out_specs=pl.BlockSpec((ng, N_KEYS, tm), lambda i: (0, 0, i)),
        compiler_params=_cparams("arbitrary"),
        name="peer_scores",
    )(x2, mod_l, wq, keys_g)


TOPK_TOKENS = 128


def _top16_rows(s, ids, big):
    vals, out_ids = [], []
    for _ in range(PEER_TOPK):
        m = jnp.max(s, axis=0, keepdims=True)
        idx = jnp.min(jnp.where(s == m, ids, big), axis=0, keepdims=True)
        vals.append(m)
        out_ids.append(idx)
        s = jnp.where(ids == idx, NEG_INF, s)
    return jnp.concatenate(vals, axis=0), jnp.concatenate(out_ids, axis=0)


def _select_rows(table, sel):
    out = jnp.zeros(sel.shape, table.dtype)
    for a in range(PEER_TOPK):
        out = jnp.where(sel == a, table[a:a + 1, :], out)
    return out


def _pair_candidates(v1, v2):
    sub = lax.broadcasted_iota(jnp.int32, (SUBLANES,) + v1.shape[1:], 0)
    cand, flat = [], []
    for b0 in (0, SUBLANES):
        cand.append(v1[0:1, :] + v2[b0:b0 + SUBLANES, :])
        flat.append(sub + b0)
    for a in range(1, SUBLANES):
        cand.append(v1[a:a + 1, :] + v2[0:SUBLANES, :])
        flat.append(sub + a * PEER_TOPK)
    cand.append(v1[SUBLANES:, :] + v2[0:1, :])
    flat.append((sub + SUBLANES) * PEER_TOPK)
    return jnp.concatenate(cand, axis=0), jnp.concatenate(flat, axis=0)


def _peer_topk_kernel(s_ref, e_ref, g_ref, et_ref, gt_ref):
    key_ids = lax.broadcasted_iota(jnp.int32, s_ref.shape[1:], 0)

    def head(hd):
        v1, i1 = _top16_rows(s_ref[2 * hd], key_ids, N_KEYS)
        v2, i2 = _top16_rows(s_ref[2 * hd + 1], key_ids, N_KEYS)
        cand, flat = _pair_candidates(v1, v2)
        cv, ci = _top16_rows(cand, flat, PEER_TOPK * PEER_TOPK)
        e1 = _select_rows(i1, ci // PEER_TOPK)
        e2 = _select_rows(i2, ci % PEER_TOPK)
        p = jnp.exp(cv - cv[0:1, :])
        gates = p / jnp.sum(p, axis=0, keepdims=True)
        rows = pl.ds(pl.multiple_of(hd * PEER_TOPK, PEER_TOPK), PEER_TOPK)
        et_ref[rows, :] = e1 * N_KEYS + e2
        gt_ref[rows, :] = gates

    def head_pair(j, carry):
        head(2 * j)
        head(2 * j + 1)
        return carry

    lax.fori_loop(0, PEER_HEADS // 2, head_pair, 0)
    e_ref[...] = et_ref[...].T
    g_ref[...] = gt_ref[...].T


def _peer_topk(s):
    ng, nk, t = s.shape
    tt = TOPK_TOKENS
    width = PEER_HEADS * PEER_TOPK
    return pl.pallas_call(
        _peer_topk_kernel,
        out_shape=(jax.ShapeDtypeStruct((t, width), jnp.int32), jax.ShapeDtypeStruct((t, width), F32)),
        grid=(t // tt,),
        in_specs=[pl.BlockSpec((ng, nk, tt), lambda i: (0, 0, i))],
        out_specs=(pl.BlockSpec((tt, width), lambda i: (i, 0)), pl.BlockSpec((tt, width), lambda i: (i, 0))),
        scratch_shapes=[pltpu.VMEM((width, tt), jnp.int32), pltpu.VMEM((width, tt), F32)],
        compiler_params=_cparams("arbitrary"),
        name="peer_topk",
    )(s)


PEER_PHASES = 4
PEER_PHASE_TOKENS = 4
PEER_TOKENS = PEER_PHASES * PEER_PHASE_TOKENS
PEER_PICKS = PEER_HEADS * PEER_TOPK


def _gelu_tanh(z):
    return 0.5 * z * (1.0 + jnp.tanh(math.sqrt(2.0 / math.pi) * (z + 0.044715 * (z * z * z))))


def _unpack_pair(w32):
    lo = lax.bitcast_convert_type(w32 << 16, F32)
    hi = lax.bitcast_convert_type(w32 & jnp.uint32(0xFFFF0000), F32)
    return lo, hi


def _peer_mix_kernel(alpha, e_cur_ref, e_nxt_ref, gates_ref, x_ref, mod_ref, g_ref, b_ref, uv_ref,
                     o_ref, buf0, buf1, buf2, buf3, h_ref, ffn_ref, sem_ref):
    i = pl.program_id(0)
    n = pl.num_programs(0)
    d = x_ref.shape[1]
    hd = d // 2
    bufs = (buf0, buf1, buf2, buf3)
    ptok = PEER_PHASE_TOKENS

    def start_row(e_ref, row, buf, sem, t, k):
        pltpu.make_async_copy(uv_ref.at[e_ref[row, k]], buf.at[t, pl.ds(k, 1), :], sem).start(priority=k % 2)

    def wait_buf(p):
        pltpu.make_async_copy(bufs[p], bufs[p], sem_ref.at[p]).wait()

    @pl.when(i == 0)
    def _():
        for p in range(2):
            for t in range(ptok):
                for k in range(PEER_PICKS):
                    start_row(e_cur_ref, p * ptok + t, bufs[p], sem_ref.at[p], t, k)

    x = x_ref[...]
    h_ref[...] = x * (1.0 + mod_ref[4:5, :]) + mod_ref[3:4, :]
    lane = lax.broadcasted_iota(jnp.int32, (SUBLANES, PEER_PICKS), 1)
    sub = lax.broadcasted_iota(jnp.int32, (SUBLANES, PEER_PICKS), 0)

    for p in range(PEER_PHASES):
        q = (p + 2) % PEER_PHASES
        ahead_ref, ahead_row0 = (e_cur_ref, (p + 2) * ptok) if p + 2 < PEER_PHASES else (e_nxt_ref, q * ptok)
        wait_buf(p)
        for t in range(ptok):
            r = p * ptok + t
            h_lo = h_ref[r:r + 1, 0:hd]
            h_hi = h_ref[r:r + 1, hd:d]
            grow = gates_ref[r:r + 1, :]
            acc_lo = jnp.zeros((SUBLANES, hd), F32)
            acc_hi = jnp.zeros((SUBLANES, hd), F32)
            for g in range(PEER_PICKS // SUBLANES):
                k0 = g * SUBLANES
                w32 = bufs[p][t, pl.ds(k0, SUBLANES), :]
                for j in range(SUBLANES):
                    start_row(ahead_ref, ahead_row0 + t, bufs[q], sem_ref.at[q], t, k0 + j)
                u_lo, u_hi = _unpack_pair(w32[:, 0:hd])
                a = jnp.sum(u_lo * h_lo + u_hi * h_hi, axis=1, keepdims=True)
                gate = jnp.sum(jnp.where(lane == sub + k0, grow, 0.0), axis=1, keepdims=True)
                w = gate * _gelu_tanh(a)
                v_lo, v_hi = _unpack_pair(w32[:, hd:d])
                acc_lo = acc_lo + w * v_lo
                acc_hi = acc_hi + w * v_hi
            ffn_ref[r:r + 1, 0:hd] = jnp.sum(acc_lo, axis=0, keepdims=True)
            ffn_ref[r:r + 1, hd:d] = jnp.sum(acc_hi, axis=0, keepdims=True)

    @pl.when(i == n - 1)
    def _():
        wait_buf(0)
        wait_buf(1)

    g2 = mod_ref[5:6, :]
    o_ref[...] = _layer_norm(alpha * x + (1.0 + g2) * ffn_ref[...], g_ref[...], b_ref[...])


def _peer_mix(experts, gates, x2, mod_l, ln_g, ln_b, uvp, seq, alpha):
    t, d = x2.shape
    tb = PEER_TOKENS
    nblk = t // tb
    per_b = seq // tb
    const = lambda i: (0, 0)
    return pl.pallas_call(
        functools.partial(_peer_mix_kernel, alpha),
        out_shape=jax.ShapeDtypeStruct((t, d), F32),
        grid=(nblk,),
        in_specs=[
            pl.BlockSpec((tb, PEER_PICKS), lambda i: (i, 0), memory_space=pltpu.SMEM),
            pl.BlockSpec((tb, PEER_PICKS), lambda i: (jnp.minimum(i + 1, nblk - 1), 0),
                         memory_space=pltpu.SMEM),
            pl.BlockSpec((tb, PEER_PICKS), lambda i: (i, 0)),
            pl.BlockSpec((tb, d), lambda i: (i, 0)),
            pl.BlockSpec((None, 6, d), lambda i: (i // per_b, 0, 0)),
            pl.BlockSpec((1, d), const),
            pl.BlockSpec((1, d), const),
            pl.BlockSpec(memory_space=pl.ANY),
        ],
        out_specs=pl.BlockSpec((tb, d), lambda i: (i, 0)),
        scratch_shapes=[pltpu.VMEM((PEER_PHASE_TOKENS, PEER_PICKS, d), jnp.uint32)] * PEER_PHASES + [
                        pltpu.VMEM((tb, d), F32),
                        pltpu.VMEM((tb, d), F32),
                        pltpu.SemaphoreType.DMA((PEER_PHASES,))],
        compiler_params=_cparams("arbitrary"),
        name="peer_mix",
    )(experts, experts, gates, x2, mod_l, ln_g, ln_b, uvp)


PACK_ROWS = 256


def _bf16_bits_high(a):
    b = lax.bitcast_convert_type(a, jnp.uint32)
    return (b + jnp.uint32(0x7FFF) + ((b >> 16) & jnp.uint32(1))) & jnp.uint32(0xFFFF0000)


def _pack_table_kernel(u_ref, v_ref, o_ref):
    hd = u_ref.shape[1] // 2
    for part, src in enumerate((u_ref, v_ref)):
        lo = _bf16_bits_high(src[:, 0:hd]) >> 16
        hi = _bf16_bits_high(src[:, hd:])
        o_ref[:, 0, part * hd:(part + 1) * hd] = lo | hi


def _pack_table(peer_u, peer_v):
    depth, n, d = peer_u.shape
    spec = pl.BlockSpec((None, PACK_ROWS, d), lambda l, i: (l, i, 0))
    return pl.pallas_call(
        _pack_table_kernel,
        out_shape=jax.ShapeDtypeStruct((depth, n, 1, d), jnp.uint32),
        grid=(depth, n // PACK_ROWS),
        in_specs=[spec, spec],
        out_specs=pl.BlockSpec((None, PACK_ROWS, 1, d), lambda l, i: (l, i, 0, 0)),
        compiler_params=_cparams("arbitrary", "arbitrary"),
        name="pack_table",
    )(peer_u, peer_v)


def kernel(x, c, ada_w, ada_b, w_in, w_gate2, b_gate, gla_norm_g, conv_w, conv_norm_g, w_out, ln1_g, ln1_b,
           peer_wq, peer_keys, peer_u, peer_v, ln2_g, ln2_b):
    bsz, seq, d = x.shape
    depth = ada_w.shape[0]
    alpha = (2.0 * depth) ** 0.25
    t = bsz * seq

    mod = _ada_mod(c, ada_w, ada_b).reshape(depth, bsz, 6, d)

    qkvr = 2 * GLA_KEY_WIDTH + 2 * GLA_WIDTH
    w_alr = jnp.pad(w_in[:, :, qkvr:qkvr + GLA_GATE_RANK], ((0, 0), (0, 0), (0, ALR_PAD - GLA_GATE_RANK)))
    w_in_p = jnp.concatenate([w_in[:, :, :qkvr], w_in[:, :, qkvr + GLA_GATE_RANK:], w_alr], axis=-1).astype(BF16)
    wg_p = jnp.pad(w_gate2, ((0, 0), (0, ALR_PAD - GLA_GATE_RANK), (0, 0)))
    w_out_b = w_out.astype(BF16)
    wq_b = peer_wq.astype(BF16)
    keys_g = peer_keys.reshape(depth, 2 * PEER_HEADS, N_KEYS, PEER_HALF).astype(BF16)
    uv = _pack_table(peer_u, peer_v)

    x2 = x.reshape(t, d)
    for l in range(depth):
        proj = _in_proj(x2, mod[l], w_in_p[l], seq)
        y = _gla_conv(proj, wg_p[l], b_gate[l][None], gla_norm_g[l][None], conv_w[l], conv_norm_g[l][None],
                      bsz, seq)
        x2 = _out_proj_ln(y, x2, mod[l], w_out_b[l], ln1_g[l][None], ln1_b[l][None], seq, alpha)
        s = _peer_scores(x2, mod[l], wq_b[l], keys_g[l], seq)
        experts, gates = _peer_topk(s)
        x2 = _peer_mix(experts, gates, x2, mod[l], ln2_g[l][None], ln2_b[l][None], uv[l], seq, alpha)
    return x2.reshape(bsz, seq, d)
```

```python
import functools
import math

import jax
import jax.numpy as jnp
from jax import lax
from jax.experimental import pallas as pl
from jax.experimental.pallas import tpu as pltpu

CHUNK = 64
GLA_HEADS = 4
GLA_DK = 128
GLA_DV = 256
GLA_KEY_WIDTH = GLA_HEADS * GLA_DK
GLA_WIDTH = GLA_HEADS * GLA_DV
GLA_GATE_RANK = 16
GLA_GATE_TAU = 16.0
CONV_WIDTH = 1024
PEER_HEADS = 8
N_KEYS = 128
PEER_TOPK = 16
PEER_HALF = 128
EPS = 1e-5

LANES = 128
SUBLANES = 8
VMEM_LIMIT_BYTES = 56 * 1024 * 1024

ALR_PAD = LANES
PROJ_WIDTH = 2 * GLA_KEY_WIDTH + 2 * GLA_WIDTH + 3 * CONV_WIDTH + ALR_PAD

F32 = jnp.float32
BF16 = jnp.bfloat16
NEG_INF = float("-inf")


def _cparams(*sem):
    return pltpu.CompilerParams(dimension_semantics=sem, vmem_limit_bytes=VMEM_LIMIT_BYTES)


def _layer_norm(z, g, b):
    mu = jnp.mean(z, axis=-1, keepdims=True)
    zc = z - mu
    var = jnp.mean(zc * zc, axis=-1, keepdims=True)
    return zc * lax.rsqrt(var + EPS) * g + b


def _rms_norm(z, g):
    return z * lax.rsqrt(jnp.mean(z * z, axis=-1, keepdims=True) + EPS) * g


def _sigmoid(z):
    return 1.0 / (1.0 + jnp.exp(-z))


def _ada_kernel(c_ref, w_ref, b_ref, o_ref):
    c = c_ref[...]
    c_act = (c * _sigmoid(c)).astype(BF16)
    o_ref[...] = jnp.dot(c_act, w_ref[...].astype(BF16), preferred_element_type=F32) + b_ref[...]


def _ada_mod(c, ada_w, ada_b):
    depth, d, n = ada_w.shape
    bsz = c.shape[0]
    tn = 1024
    return pl.pallas_call(
        _ada_kernel,
        out_shape=jax.ShapeDtypeStruct((depth, bsz, n), F32),
        grid=(depth, n // tn),
        in_specs=[
            pl.BlockSpec((bsz, d), lambda l, j: (0, 0)),
            pl.BlockSpec((None, d, tn), lambda l, j: (l, 0, j)),
            pl.BlockSpec((None, 1, tn), lambda l, j: (l, 0, j)),
        ],
        out_specs=pl.BlockSpec((None, bsz, tn), lambda l, j: (l, 0, j)),
        compiler_params=_cparams("arbitrary", "arbitrary"),
        name="ada_mod",
    )(c, ada_w, ada_b.reshape(depth, 1, n))


def _in_proj_kernel(x_ref, mod_ref, w_ref, o_ref, h_ref):
    @pl.when(pl.program_id(1) == 0)
    def _():
        sh = mod_ref[0:1, :]
        sc = mod_ref[1:2, :]
        h_ref[...] = (x_ref[...] * (1.0 + sc) + sh).astype(BF16)

    o_ref[...] = jnp.dot(h_ref[...], w_ref[...], preferred_element_type=F32)


def _in_proj(x2, mod_l, w_in_p, seq):
    t, d = x2.shape
    n = w_in_p.shape[1]
    tm, tn = 512, 896
    per_b = seq // tm
    return pl.pallas_call(
        _in_proj_kernel,
        out_shape=jax.ShapeDtypeStruct((t, n), F32),
        grid=(t // tm, n // tn),
        in_specs=[
            pl.BlockSpec((tm, d), lambda i, j: (i, 0)),
            pl.BlockSpec((None, 6, d), lambda i, j: (i // per_b, 0, 0)),
            pl.BlockSpec((d, tn), lambda i, j: (0, j)),
        ],
        out_specs=pl.BlockSpec((tm, tn), lambda i, j: (i, j)),
        scratch_shapes=[pltpu.VMEM((tm, d), BF16)],
        compiler_params=_cparams("arbitrary", "arbitrary"),
        name="in_proj",
    )(x2, mod_l, w_in_p)


GLA_ROWS = 256


def _gla_conv_kernel(q_ref, k_ref, v_ref, r_ref, cb_ref, cc_ref, ch_ref, alr_ref,
                     wg_ref, bg_ref, gng_ref, cw_ref, cng_ref, y_ref, state_ref, carry_ref):
    @pl.when(pl.program_id(1) == 0)
    def _():
        state_ref[...] = jnp.zeros_like(state_ref)
        carry_ref[...] = jnp.zeros_like(carry_ref)

    logit = jnp.dot(alr_ref[...], wg_ref[...], preferred_element_type=F32,
                    precision=lax.Precision.HIGHEST) + bg_ref[...]
    log_a = (jnp.minimum(logit, 0.0) - jnp.log(1.0 + jnp.exp(-jnp.abs(logit)))) * (1.0 / GLA_GATE_TAU)
    row = lax.broadcasted_iota(jnp.int32, (CHUNK, CHUNK), 0)
    col = lax.broadcasted_iota(jnp.int32, (CHUNK, CHUNK), 1)
    tri = (col <= row).astype(F32)
    gng = gng_ref[...]
    scale = GLA_DK ** -0.5
    for c in range(GLA_ROWS // CHUNK):
        rows = slice(c * CHUNK, (c + 1) * CHUNK)
        cum = jnp.dot(tri, log_a[rows], preferred_element_type=F32, precision=lax.Precision.HIGHEST)
        total = cum[CHUNK - 1:CHUNK, :]
        k_dec = (k_ref[rows, :] * jnp.exp(total - cum)).astype(BF16)
        dec = jnp.exp(total)
        qs = (q_ref[rows, :] * scale).astype(BF16)
        for h in range(GLA_HEADS):
            ks = slice(h * GLA_DK, (h + 1) * GLA_DK)
            vs = slice(h * GLA_DV, (h + 1) * GLA_DV)
            vh = v_ref[rows, vs].astype(BF16)
            upd = lax.dot_general(vh, k_dec[:, ks], (((0,), (0,)), ((), ())), preferred_element_type=F32)
            st = state_ref[h] * dec[:, ks] + upd
            state_ref[h] = st
            o = lax.dot_general(qs[:, ks], st.astype(BF16), (((1,), (1,)), ((), ())),
                                preferred_element_type=F32)
            rh = r_ref[rows, vs]
            y_ref[rows, vs] = (_rms_norm(o, gng) * (rh * _sigmoid(rh))).astype(y_ref.dtype)

    u = cc_ref[...] * ch_ref[...]
    prev = carry_ref[...]
    ridx = lax.broadcasted_iota(jnp.int32, (GLA_ROWS, 1), 0)
    u1 = jnp.where(ridx == 0, prev[SUBLANES - 1:SUBLANES, :], pltpu.roll(u, 1, 0))
    u2 = jnp.where(ridx == 0, prev[SUBLANES - 2:SUBLANES - 1, :],
                   jnp.where(ridx == 1, prev[SUBLANES - 1:SUBLANES, :], pltpu.roll(u, 2, 0)))
    conv = cw_ref[0:1, :] * u2 + cw_ref[1:2, :] * u1 + cw_ref[2:3, :] * u
    carry_ref[...] = u[GLA_ROWS - SUBLANES:, :]
    y_ref[:, GLA_WIDTH:] = _rms_norm(cb_ref[...] * conv, cng_ref[...]).astype(y_ref.dtype)


def _gla_conv(proj, wg_p, bg, gng, cw, cng, bsz, seq):
    t = proj.shape[0]
    nblk = seq // GLA_ROWS
    rmap = lambda width_blocks: (lambda b, j: (b * nblk + j, width_blocks))
    kw, vw = GLA_KEY_WIDTH, GLA_WIDTH
    const = lambda b, j: (0, 0)
    return pl.pallas_call(
        _gla_conv_kernel,
        out_shape=jax.ShapeDtypeStruct((t, GLA_WIDTH + CONV_WIDTH), BF16),
        grid=(bsz, nblk),
        in_specs=[
            pl.BlockSpec((GLA_ROWS, kw), rmap(0)),
            pl.BlockSpec((GLA_ROWS, kw), rmap(1)),
            pl.BlockSpec((GLA_ROWS, vw), rmap(1)),
            pl.BlockSpec((GLA_ROWS, vw), rmap(2)),
            pl.BlockSpec((GLA_ROWS, CONV_WIDTH), rmap(3)),
            pl.BlockSpec((GLA_ROWS, CONV_WIDTH), rmap(4)),
            pl.BlockSpec((GLA_ROWS, CONV_WIDTH), rmap(5)),
            pl.BlockSpec((GLA_ROWS, ALR_PAD), rmap(6144 // ALR_PAD)),
            pl.BlockSpec((ALR_PAD, kw), const),
            pl.BlockSpec((1, kw), const),
            pl.BlockSpec((1, GLA_DV), const),
            pl.BlockSpec((3, CONV_WIDTH), const),
            pl.BlockSpec((1, CONV_WIDTH), const),
        ],
        out_specs=pl.BlockSpec((GLA_ROWS, GLA_WIDTH + CONV_WIDTH), lambda b, j: (b * nblk + j, 0)),
        scratch_shapes=[pltpu.VMEM((GLA_HEADS, GLA_DV, GLA_DK), F32),
                        pltpu.VMEM((SUBLANES, CONV_WIDTH), F32)],
        compiler_params=_cparams("arbitrary", "arbitrary"),
        name="gla_conv",
    )(proj, proj, proj, proj, proj, proj, proj, proj, wg_p, bg, gng, cw, cng)


def _out_proj_ln_kernel(alpha, y_ref, x_ref, mod_ref, w_ref, g_ref, b_ref, o_ref):
    mix = jnp.dot(y_ref[...], w_ref[...], preferred_element_type=F32)
    g1 = mod_ref[2:3, :]
    o_ref[...] = _layer_norm(alpha * x_ref[...] + (1.0 + g1) * mix, g_ref[...], b_ref[...])


def _out_proj_ln(y, x2, mod_l, w_out, ln_g, ln_b, seq, alpha):
    t, d = x2.shape
    tm = 512
    per_b = seq // tm
    const = lambda i: (0, 0)
    return pl.pallas_call(
        functools.partial(_out_proj_ln_kernel, alpha),
        out_shape=jax.ShapeDtypeStruct((t, d), F32),
        grid=(t // tm,),
        in_specs=[
            pl.BlockSpec((tm, y.shape[1]), lambda i: (i, 0)),
            pl.BlockSpec((tm, d), lambda i: (i, 0)),
            pl.BlockSpec((None, 6, d), lambda i: (i // per_b, 0, 0)),
            pl.BlockSpec(w_out.shape, const),
            pl.BlockSpec((1, d), const),
            pl.BlockSpec((1, d), const),
        ],
        out_specs=pl.BlockSpec((tm, d), lambda i: (i, 0)),
        compiler_params=_cparams("arbitrary"),
        name="out_proj_ln",
    )(y, x2, mod_l, w_out, ln_g, ln_b)


def _peer_scores_kernel(x_ref, mod_ref, wq_ref, keys_ref, s_ref):
    sh = mod_ref[3:4, :]
    sc = mod_ref[4:5, :]
    h = (x_ref[...] * (1.0 + sc) + sh).astype(BF16)
    q = jnp.dot(h, wq_ref[...], preferred_element_type=F32).astype(BF16)
    for g in range(2 * PEER_HEADS):
        qg = q[:, g * PEER_HALF:(g + 1) * PEER_HALF]
        s_ref[g] = lax.dot_general(keys_ref[g], qg, (((1,), (1,)), ((), ())), preferred_element_type=F32)


def _peer_scores(x2, mod_l, wq, keys_g, seq):
    t, d = x2.shape
    tm = 512
    per_b = seq // tm
    ng = 2 * PEER_HEADS
    return pl.pallas_call(
        _peer_scores_kernel,
        out_shape=jax.ShapeDtypeStruct((ng, N_KEYS, t), F32),
        grid=(t // tm,),
        in_specs=[
            pl.BlockSpec((tm, d), lambda i: (i, 0)),
            pl.BlockSpec((None, 6, d), lambda i: (i // per_b, 0, 0)),
            pl.BlockSpec(wq.shape, lambda i: (0, 0)),
            pl.BlockSpec(keys_g.shape, lambda i: (0, 0, 0)),
        ],
        out_specs=pl.BlockSpec((ng, N_KEYS, tm), lambda i: (0, 0, i)),
        compiler_params=_cparams("arbitrary"),
        name="peer_scores",
    )(x2, mod_l, wq, keys_g)


TOPK_TOKENS = 128


def _top16_rows(s, ids, big):
    vals, out_ids = [], []
    for _ in range(PEER_TOPK):
        m = jnp.max(s, axis=0, keepdims=True)
        idx = jnp.min(jnp.where(s == m, ids, big), axis=0, keepdims=True)
        vals.append(m)
        out_ids.append(idx)
        s = jnp.where(ids == idx, NEG_INF, s)
    return jnp.concatenate(vals, axis=0), jnp.concatenate(out_ids, axis=0)


def _select_rows(table, sel):
    out = jnp.zeros(sel.shape, table.dtype)
    for a in range(PEER_TOPK):
        out = jnp.where(sel == a, table[a:a + 1, :], out)
    return out


def _pair_candidates(v1, v2):
    sub = lax.broadcasted_iota(jnp.int32, (SUBLANES,) + v1.shape[1:], 0)
    cand, flat = [], []
    for b0 in (0, SUBLANES):
        cand.append(v1[0:1, :] + v2[b0:b0 + SUBLANES, :])
        flat.append(sub + b0)
    for a in range(1, SUBLANES):
        cand.append(v1[a:a + 1, :] + v2[0:SUBLANES, :])
        flat.append(sub + a * PEER_TOPK)
    cand.append(v1[SUBLANES:, :] + v2[0:1, :])
    flat.append((sub + SUBLANES) * PEER_TOPK)
    return jnp.concatenate(cand, axis=0), jnp.concatenate(flat, axis=0)


def _peer_topk_kernel(s_ref, e_ref, g_ref, et_ref, gt_ref):
    key_ids = lax.broadcasted_iota(jnp.int32, s_ref.shape[1:], 0)

    def head(hd):
        v1, i1 = _top16_rows(s_ref[2 * hd], key_ids, N_KEYS)
        v2, i2 = _top16_rows(s_ref[2 * hd + 1], key_ids, N_KEYS)
        cand, flat = _pair_candidates(v1, v2)
        cv, ci = _top16_rows(cand, flat, PEER_TOPK * PEER_TOPK)
        e1 = _select_rows(i1, ci // PEER_TOPK)
        e2 = _select_rows(i2, ci % PEER_TOPK)
        p = jnp.exp(cv - cv[0:1, :])
        gates = p / jnp.sum(p, axis=0, keepdims=True)
        rows = pl.ds(pl.multiple_of(hd * PEER_TOPK, PEER_TOPK), PEER_TOPK)
        et_ref[rows, :] = e1 * N_KEYS + e2
        gt_ref[rows, :] = gates

    def head_pair(j, carry):
        head(2 * j)
        head(2 * j + 1)
        return carry

    lax.fori_loop(0, PEER_HEADS // 2, head_pair, 0)
    e_ref[...] = et_ref[...].T
    g_ref[...] = gt_ref[...].T


def _peer_topk(s):
    ng, nk, t = s.shape
    tt = TOPK_TOKENS
    width = PEER_HEADS * PEER_TOPK
    return pl.pallas_call(
        _peer_topk_kernel,
        out_shape=(jax.ShapeDtypeStruct((t, width), jnp.int32), jax.ShapeDtypeStruct((t, width), F32)),
        grid=(t // tt,),
        in_specs=[pl.BlockSpec((ng, nk, tt), lambda i: (0, 0, i))],
        out_specs=(pl.BlockSpec((tt, width), lambda i: (i, 0)), pl.BlockSpec((tt, width), lambda i: (i, 0))),
        scratch_shapes=[pltpu.VMEM((width, tt), jnp.int32), pltpu.VMEM((width, tt), F32)],
        compiler_params=_cparams("arbitrary"),
        name="peer_topk",
    )(s)


PEER_PHASES = 4
PEER_PHASE_TOKENS = 4
PEER_TOKENS = PEER_PHASES * PEER_PHASE_TOKENS
PEER_PICKS = PEER_HEADS * PEER_TOPK


def _gelu_tanh(z):
    return 0.5 * z * (1.0 + jnp.tanh(math.sqrt(2.0 / math.pi) * (z + 0.044715 * (z * z * z))))


def _unpack_pair(w32):
    lo = lax.bitcast_convert_type(w32 << 16, F32)
    hi = lax.bitcast_convert_type(w32 & jnp.uint32(0xFFFF0000), F32)
    return lo, hi


def _sublane_sums(tiles):
    assert len(tiles) == SUBLANES
    sub = lax.broadcasted_iota(jnp.int32, tiles[0].shape, 0)
    dist = SUBLANES // 2
    while dist >= 1:
        keep_low = (sub & dist) == 0
        nxt = []
        for j in range(len(tiles) // 2):
            lo_t, hi_t = tiles[j], tiles[j + len(tiles) // 2]
            nxt.append(jnp.where(keep_low, lo_t + pltpu.roll(lo_t, SUBLANES - dist, 0),
                                 hi_t + pltpu.roll(hi_t, dist, 0)))
        tiles = nxt
        dist //= 2
    return tiles[0]


def _peer_mix_kernel(alpha, e_cur_ref, e_nxt_ref, gates_ref, x_ref, mod_ref, g_ref, b_ref, uv_ref,
                     o_ref, buf0, buf1, buf2, buf3, h_ref, ffn_ref, sem_ref):
    i = pl.program_id(0)
    n = pl.num_programs(0)
    d = x_ref.shape[1]
    hd = d // 2
    bufs = (buf0, buf1, buf2, buf3)
    ptok = PEER_PHASE_TOKENS

    def start_row(e_ref, row, buf, sem, t, k):
        pltpu.make_async_copy(uv_ref.at[e_ref[row, k]], buf.at[t, k], sem).start(priority=k % 2)

    def wait_buf(p):
        pltpu.make_async_copy(bufs[p], bufs[p], sem_ref.at[p]).wait()

    @pl.when(i == 0)
    def _():
        for p in range(2):
            for t in range(ptok):
                for k in range(PEER_PICKS):
                    start_row(e_cur_ref, p * ptok + t, bufs[p], sem_ref.at[p], t, k)

    x = x_ref[...]
    h = x * (1.0 + mod_ref[4:5, :]) + mod_ref[3:4, :]
    for s in range(d // LANES):
        h_ref[:, s, :] = h[:, s * LANES:(s + 1) * LANES]
    lane = lax.broadcasted_iota(jnp.int32, (SUBLANES, PEER_PICKS), 1)
    sub = lax.broadcasted_iota(jnp.int32, (SUBLANES, PEER_PICKS), 0)

    for p in range(PEER_PHASES):
        q = (p + 2) % PEER_PHASES
        ahead_ref, ahead_row0 = (e_cur_ref, (p + 2) * ptok) if p + 2 < PEER_PHASES else (e_nxt_ref, q * ptok)
        wait_buf(p)
        for t in range(ptok):
            r = p * ptok + t
            h_lo = h_ref[r, 0:SUBLANES, :]
            h_hi = h_ref[r, SUBLANES:, :]
            grow = gates_ref[r:r + 1, :]
            y_lo = jnp.zeros((SUBLANES, LANES), F32)
            y_hi = jnp.zeros((SUBLANES, LANES), F32)
            for g in range(PEER_PICKS // SUBLANES):
                k0 = g * SUBLANES
                prods = []
                for j in range(SUBLANES):
                    u_lo, u_hi = _unpack_pair(bufs[p][t, k0 + j, 0:SUBLANES, :])
                    start_row(ahead_ref, ahead_row0 + t, bufs[q], sem_ref.at[q], t, k0 + j)
                    prods.append(u_lo * h_lo + u_hi * h_hi)
                a = jnp.sum(_sublane_sums(prods), axis=1, keepdims=True)
                gate = jnp.sum(jnp.where(lane == sub + k0, grow, 0.0), axis=1, keepdims=True)
                w = jnp.broadcast_to(gate * _gelu_tanh(a), (SUBLANES, LANES))
                for j in range(SUBLANES):
                    v_lo, v_hi = _unpack_pair(bufs[p][t, k0 + j, SUBLANES:, :])
                    wj = jnp.broadcast_to(w[j:j + 1, :], (SUBLANES, LANES))
                    y_lo = y_lo + wj * v_lo
                    y_hi = y_hi + wj * v_hi
            ffn_ref[r, 0:SUBLANES, :] = y_lo
            ffn_ref[r, SUBLANES:, :] = y_hi

    @pl.when(i == n - 1)
    def _():
        wait_buf(0)
        wait_buf(1)

    ffn = jnp.concatenate([ffn_ref[:, s, :] for s in range(d // LANES)], axis=1)
    g2 = mod_ref[5:6, :]
    o_ref[...] = _layer_norm(alpha * x + (1.0 + g2) * ffn, g_ref[...], b_ref[...])


def _peer_mix(experts, gates, x2, mod_l, ln_g, ln_b, uvp, seq, alpha):
    t, d = x2.shape
    tb = PEER_TOKENS
    nblk = t // tb
    per_b = seq // tb
    const = lambda i: (0, 0)
    return pl.pallas_call(
        functools.partial(_peer_mix_kernel, alpha),
        out_shape=jax.ShapeDtypeStruct((t, d), F32),
        grid=(nblk,),
        in_specs=[
            pl.BlockSpec((tb, PEER_PICKS), lambda i: (i, 0), memory_space=pltpu.SMEM),
            pl.BlockSpec((tb, PEER_PICKS), lambda i: (jnp.minimum(i + 1, nblk - 1), 0),
                         memory_space=pltpu.SMEM),
            pl.BlockSpec((tb, PEER_PICKS), lambda i: (i, 0)),
            pl.BlockSpec((tb, d), lambda i: (i, 0)),
            pl.BlockSpec((None, 6, d), lambda i: (i // per_b, 0, 0)),
            pl.BlockSpec((1, d), const),
            pl.BlockSpec((1, d), const),
            pl.BlockSpec(memory_space=pl.ANY),
        ],
        out_specs=pl.BlockSpec((tb, d), lambda i: (i, 0)),
        scratch_shapes=[pltpu.VMEM((PEER_PHASE_TOKENS, PEER_PICKS, d // LANES, LANES), jnp.uint32)] * PEER_PHASES + [
                        pltpu.VMEM((tb, d // LANES, LANES), F32),
                        pltpu.VMEM((tb, d // LANES, LANES), F32),
                        pltpu.SemaphoreType.DMA((PEER_PHASES,))],
        compiler_params=_cparams("arbitrary"),
        name="peer_mix",
    )(experts, experts, gates, x2, mod_l, ln_g, ln_b, uvp)


PACK_ROWS = 256


def _bf16_bits_high(a):
    b = lax.bitcast_convert_type(a, jnp.uint32)
    return (b + jnp.uint32(0x7FFF) + ((b >> 16) & jnp.uint32(1))) & jnp.uint32(0xFFFF0000)


def _pack_table_kernel(u_ref, v_ref, o_ref):
    hd = u_ref.shape[1] // 2
    tiles = hd // LANES
    for part, src in enumerate((u_ref, v_ref)):
        words = (_bf16_bits_high(src[:, 0:hd]) >> 16) | _bf16_bits_high(src[:, hd:])
        for s in range(tiles):
            o_ref[:, part * tiles + s, :] = words[:, s * LANES:(s + 1) * LANES]


def _pack_table(peer_u, peer_v):
    depth, n, d = peer_u.shape
    spec = pl.BlockSpec((None, PACK_ROWS, d), lambda l, i: (l, i, 0))
    return pl.pallas_call(
        _pack_table_kernel,
        out_shape=jax.ShapeDtypeStruct((depth, n, d // LANES, LANES), jnp.uint32),
        grid=(depth, n // PACK_ROWS),
        in_specs=[spec, spec],
        out_specs=pl.BlockSpec((None, PACK_ROWS, d // LANES, LANES), lambda l, i: (l, i, 0, 0)),
        compiler_params=_cparams("arbitrary", "arbitrary"),
        name="pack_table",
    )(peer_u, peer_v)


def kernel(x, c, ada_w, ada_b, w_in, w_gate2, b_gate, gla_norm_g, conv_w, conv_norm_g, w_out, ln1_g, ln1_b,
           peer_wq, peer_keys, peer_u, peer_v, ln2_g, ln2_b):
    bsz, seq, d = x.shape
    depth = ada_w.shape[0]
    alpha = (2.0 * depth) ** 0.25
    t = bsz * seq

    mod = _ada_mod(c, ada_w, ada_b).reshape(depth, bsz, 6, d)

    qkvr = 2 * GLA_KEY_WIDTH + 2 * GLA_WIDTH
    w_alr = jnp.pad(w_in[:, :, qkvr:qkvr + GLA_GATE_RANK], ((0, 0), (0, 0), (0, ALR_PAD - GLA_GATE_RANK)))
    w_in_p = jnp.concatenate([w_in[:, :, :qkvr], w_in[:, :, qkvr + GLA_GATE_RANK:], w_alr], axis=-1).astype(BF16)
    wg_p = jnp.pad(w_gate2, ((0, 0), (0, ALR_PAD - GLA_GATE_RANK), (0, 0)))
    w_out_b = w_out.astype(BF16)
    wq_b = peer_wq.astype(BF16)
    keys_g = peer_keys.reshape(depth, 2 * PEER_HEADS, N_KEYS, PEER_HALF).astype(BF16)
    uv = _pack_table(peer_u, peer_v)

    x2 = x.reshape(t, d)
    for l in range(depth):
        proj = _in_proj(x2, mod[l], w_in_p[l], seq)
        y = _gla_conv(proj, wg_p[l], b_gate[l][None], gla_norm_g[l][None], conv_w[l], conv_norm_g[l][None],
                      bsz, seq)
        x2 = _out_proj_ln(y, x2, mod[l], w_out_b[l], ln1_g[l][None], ln1_b[l][None], seq, alpha)
        s = _peer_scores(x2, mod[l], wq_b[l], keys_g[l], seq)
        experts, gates = _peer_topk(s)
        x2 = _peer_mix(experts, gates, x2, mod[l], ln2_g[l][None], ln2_b[l][None], uv[l], seq, alpha)
    return x2.reshape(bsz, seq, d)
```

```python
import functools
import math

import jax
import jax.numpy as jnp
from jax import lax
from jax.experimental import pallas as pl
from jax.experimental.pallas import tpu as pltpu

CHUNK = 64
GLA_HEADS = 4
GLA_DK = 128
GLA_DV = 256
GLA_KEY_WIDTH = GLA_HEADS * GLA_DK
GLA_WIDTH = GLA_HEADS * GLA_DV
GLA_GATE_RANK = 16
GLA_GATE_TAU = 16.0
CONV_WIDTH = 1024
PEER_HEADS = 8
N_KEYS = 128
PEER_TOPK = 16
PEER_HALF = 128
EPS = 1e-5

LANES = 128
SUBLANES = 8
VMEM_LIMIT_BYTES = 56 * 1024 * 1024

ALR_PAD = LANES
PROJ_WIDTH = 2 * GLA_KEY_WIDTH + 2 * GLA_WIDTH + 3 * CONV_WIDTH + ALR_PAD
PROJ_SLICE = 896
PROJ_GROUP_STEPS = PROJ_WIDTH // PROJ_SLICE + 1

F32 = jnp.float32
BF16 = jnp.bfloat16
NEG_INF = float("-inf")


def _cparams(*sem):
    return pltpu.CompilerParams(dimension_semantics=sem, vmem_limit_bytes=VMEM_LIMIT_BYTES)


def _layer_norm(z, g, b):
    mu = jnp.mean(z, axis=-1, keepdims=True)
    zc = z - mu
    var = jnp.mean(zc * zc, axis=-1, keepdims=True)
    return zc * lax.rsqrt(var + EPS) * g + b


def _rms_norm(z, g):
    return z * lax.rsqrt(jnp.mean(z * z, axis=-1, keepdims=True) + EPS) * g


def _sigmoid(z):
    return 1.0 / (1.0 + jnp.exp(-z))


def _ada_kernel(c_ref, w_ref, b_ref, o_ref):
    c = c_ref[...]
    c_act = (c * _sigmoid(c)).astype(BF16)
    o_ref[...] = jnp.dot(c_act, w_ref[...].astype(BF16), preferred_element_type=F32) + b_ref[...]


def _ada_mod(c, ada_w, ada_b):
    depth, d, n = ada_w.shape
    bsz = c.shape[0]
    tn = 1024
    return pl.pallas_call(
        _ada_kernel,
        out_shape=jax.ShapeDtypeStruct((depth, bsz, n), F32),
        grid=(depth, n // tn),
        in_specs=[
            pl.BlockSpec((bsz, d), lambda l, j: (0, 0)),
            pl.BlockSpec((None, d, tn), lambda l, j: (l, 0, j)),
            pl.BlockSpec((None, 1, tn), lambda l, j: (l, 0, j)),
        ],
        out_specs=pl.BlockSpec((None, bsz, tn), lambda l, j: (l, 0, j)),
        compiler_params=_cparams("arbitrary", "arbitrary"),
        name="ada_mod",
    )(c, ada_w, ada_b.reshape(depth, 1, n))


def _in_proj_kernel(x_ref, mod_ref, w_ref, o_ref, h_ref):
    @pl.when(pl.program_id(1) == 0)
    def _():
        sh = mod_ref[0:1, :]
        sc = mod_ref[1:2, :]
        h_ref[...] = (x_ref[...] * (1.0 + sc) + sh).astype(BF16)

    o_ref[...] = jnp.dot(h_ref[...], w_ref[...], preferred_element_type=F32)


def _in_proj(x2, mod_l, w_in_p, seq):
    t, d = x2.shape
    n = w_in_p.shape[1]
    tm, tn = min(1024, seq), PROJ_SLICE
    per_b = seq // tm
    return pl.pallas_call(
        _in_proj_kernel,
        out_shape=jax.ShapeDtypeStruct((t, n), F32),
        grid=(t // tm, n // tn),
        in_specs=[
            pl.BlockSpec((tm, d), lambda i, j: (i, 0)),
            pl.BlockSpec((None, 6, d), lambda i, j: (i // per_b, 0, 0)),
            pl.BlockSpec((d, tn), lambda i, j: (0, j)),
        ],
        out_specs=pl.BlockSpec((tm, tn), lambda i, j: (i, j)),
        scratch_shapes=[pltpu.VMEM((tm, d), BF16)],
        compiler_params=_cparams("arbitrary", "arbitrary"),
        name="in_proj",
    )(x2, mod_l, w_in_p)


GLA_ROWS = 256


def _gla_conv_kernel(q_ref, k_ref, v_ref, r_ref, cb_ref, cc_ref, ch_ref, alr_ref,
                     wg_ref, bg_ref, gng_ref, cw_ref, cng_ref, y_ref, state_ref, carry_ref):
    @pl.when(pl.program_id(1) == 0)
    def _():
        state_ref[...] = jnp.zeros_like(state_ref)
        carry_ref[...] = jnp.zeros_like(carry_ref)

    logit = jnp.dot(alr_ref[...], wg_ref[...], preferred_element_type=F32,
                    precision=lax.Precision.HIGHEST) + bg_ref[...]
    log_a = (jnp.minimum(logit, 0.0) - jnp.log(1.0 + jnp.exp(-jnp.abs(logit)))) * (1.0 / GLA_GATE_TAU)
    row = lax.broadcasted_iota(jnp.int32, (CHUNK, CHUNK), 0)
    col = lax.broadcasted_iota(jnp.int32, (CHUNK, CHUNK), 1)
    tri = (col <= row).astype(F32)
    gng = gng_ref[...]
    scale = GLA_DK ** -0.5
    for c in range(GLA_ROWS // CHUNK):
        rows = slice(c * CHUNK, (c + 1) * CHUNK)
        cum = jnp.dot(tri, log_a[rows], preferred_element_type=F32, precision=lax.Precision.HIGHEST)
        total = cum[CHUNK - 1:CHUNK, :]
        k_dec = (k_ref[rows, :] * jnp.exp(total - cum)).astype(BF16)
        dec = jnp.exp(total)
        qs = (q_ref[rows, :] * scale).astype(BF16)
        for h in range(GLA_HEADS):
            ks = slice(h * GLA_DK, (h + 1) * GLA_DK)
            vs = slice(h * GLA_DV, (h + 1) * GLA_DV)
            vh = v_ref[rows, vs].astype(BF16)
            upd = lax.dot_general(vh, k_dec[:, ks], (((0,), (0,)), ((), ())), preferred_element_type=F32)
            st = state_ref[h] * dec[:, ks] + upd
            state_ref[h] = st
            o = lax.dot_general(qs[:, ks], st.astype(BF16), (((1,), (1,)), ((), ())),
                                preferred_element_type=F32)
            rh = r_ref[rows, vs]
            y_ref[rows, vs] = (_rms_norm(o, gng) * (rh * _sigmoid(rh))).astype(y_ref.dtype)

    u = cc_ref[...] * ch_ref[...]
    prev = carry_ref[...]
    ridx = lax.broadcasted_iota(jnp.int32, (GLA_ROWS, 1), 0)
    u1 = jnp.where(ridx == 0, prev[SUBLANES - 1:SUBLANES, :], pltpu.roll(u, 1, 0))
    u2 = jnp.where(ridx == 0, prev[SUBLANES - 2:SUBLANES - 1, :],
                   jnp.where(ridx == 1, prev[SUBLANES - 1:SUBLANES, :], pltpu.roll(u, 2, 0)))
    conv = cw_ref[0:1, :] * u2 + cw_ref[1:2, :] * u1 + cw_ref[2:3, :] * u
    carry_ref[...] = u[GLA_ROWS - SUBLANES:, :]
    y_ref[:, GLA_WIDTH:] = _rms_norm(cb_ref[...] * conv, cng_ref[...]).astype(y_ref.dtype)


def _gla_conv(proj, wg_p, bg, gng, cw, cng, bsz, seq):
    t = proj.shape[0]
    nblk = seq // GLA_ROWS
    rmap = lambda width_blocks: (lambda b, j: (b * nblk + j, width_blocks))
    kw, vw = GLA_KEY_WIDTH, GLA_WIDTH
    const = lambda b, j: (0, 0)
    return pl.pallas_call(
        _gla_conv_kernel,
        out_shape=jax.ShapeDtypeStruct((t, GLA_WIDTH + CONV_WIDTH), BF16),
        grid=(bsz, nblk),
        in_specs=[
            pl.BlockSpec((GLA_ROWS, kw), rmap(0)),
            pl.BlockSpec((GLA_ROWS, kw), rmap(1)),
            pl.BlockSpec((GLA_ROWS, vw), rmap(1)),
            pl.BlockSpec((GLA_ROWS, vw), rmap(2)),
            pl.BlockSpec((GLA_ROWS, CONV_WIDTH), rmap(3)),
            pl.BlockSpec((GLA_ROWS, CONV_WIDTH), rmap(4)),
            pl.BlockSpec((GLA_ROWS, CONV_WIDTH), rmap(5)),
            pl.BlockSpec((GLA_ROWS, ALR_PAD), rmap(6144 // ALR_PAD)),
            pl.BlockSpec((ALR_PAD, kw), const),
            pl.BlockSpec((1, kw), const),
            pl.BlockSpec((1, GLA_DV), const),
            pl.BlockSpec((3, CONV_WIDTH), const),
            pl.BlockSpec((1, CONV_WIDTH), const),
        ],
        out_specs=pl.BlockSpec((GLA_ROWS, GLA_WIDTH + CONV_WIDTH), lambda b, j: (b * nblk + j, 0)),
        scratch_shapes=[pltpu.VMEM((GLA_HEADS, GLA_DV, GLA_DK), F32),
                        pltpu.VMEM((SUBLANES, CONV_WIDTH), F32)],
        compiler_params=_cparams("arbitrary", "arbitrary"),
        name="gla_conv",
    )(proj, proj, proj, proj, proj, proj, proj, proj, wg_p, bg, gng, cw, cng)


def _out_proj_ln_kernel(alpha, y_ref, x_ref, mod_ref, w_ref, g_ref, b_ref, o_ref):
    mix = jnp.dot(y_ref[...], w_ref[...], preferred_element_type=F32)
    g1 = mod_ref[2:3, :]
    o_ref[...] = _layer_norm(alpha * x_ref[...] + (1.0 + g1) * mix, g_ref[...], b_ref[...])


def _out_proj_ln(y, x2, mod_l, w_out, ln_g, ln_b, seq, alpha):
    t, d = x2.shape
    tm = 512
    per_b = seq // tm
    const = lambda i: (0, 0)
    return pl.pallas_call(
        functools.partial(_out_proj_ln_kernel, alpha),
        out_shape=jax.ShapeDtypeStruct((t, d), F32),
        grid=(t // tm,),
        in_specs=[
            pl.BlockSpec((tm, y.shape[1]), lambda i: (i, 0)),
            pl.BlockSpec((tm, d), lambda i: (i, 0)),
            pl.BlockSpec((None, 6, d), lambda i: (i // per_b, 0, 0)),
            pl.BlockSpec(w_out.shape, const),
            pl.BlockSpec((1, d), const),
            pl.BlockSpec((1, d), const),
        ],
        out_specs=pl.BlockSpec((tm, d), lambda i: (i, 0)),
        compiler_params=_cparams("arbitrary"),
        name="out_proj_ln",
    )(y, x2, mod_l, w_out, ln_g, ln_b)


def _peer_scores_kernel(x_ref, mod_ref, wq_ref, keys_ref, s_ref):
    sh = mod_ref[3:4, :]
    sc = mod_ref[4:5, :]
    h = (x_ref[...] * (1.0 + sc) + sh).astype(BF16)
    q = jnp.dot(h, wq_ref[...], preferred_element_type=F32).astype(BF16)
    for g in range(2 * PEER_HEADS):
        qg = q[:, g * PEER_HALF:(g + 1) * PEER_HALF]
        s_ref[g] = lax.dot_general(keys_ref[g], qg, (((1,), (1,)), ((), ())), preferred_element_type=F32)


def _peer_scores(x2, mod_l, wq, keys_g, seq):
    t, d = x2.shape
    tm = 512
    per_b = seq // tm
    ng = 2 * PEER_HEADS
    return pl.pallas_call(
        _peer_scores_kernel,
        out_shape=jax.ShapeDtypeStruct((ng, N_KEYS, t), F32),
        grid=(t // tm,),
        in_specs=[
            pl.BlockSpec((tm, d), lambda i: (i, 0)),
            pl.BlockSpec((None, 6, d), lambda i: (i // per_b, 0, 0)),
            pl.BlockSpec(wq.shape, lambda i: (0, 0)),
            pl.BlockSpec(keys_g.shape, lambda i: (0, 0, 0)),
        ],
        out_specs=pl.BlockSpec((ng, N_KEYS, tm), lambda i: (0, 0, i)),
        compiler_params=_cparams("arbitrary"),
        name="peer_scores",
    )(x2, mod_l, wq, keys_g)


TOPK_TOKENS = 128


def _top16_rows(s, ids, big):
    vals, out_ids = [], []
    for _ in range(PEER_TOPK):
        m = jnp.max(s, axis=0, keepdims=True)
        idx = jnp.min(jnp.where(s == m, ids, big), axis=0, keepdims=True)
        vals.append(m)
        out_ids.append(idx)
        s = jnp.where(ids == idx, NEG_INF, s)
    return jnp.concatenate(vals, axis=0), jnp.concatenate(out_ids, axis=0)


def _select_rows(table, sel):
    out = jnp.zeros(sel.shape, table.dtype)
    for a in range(PEER_TOPK):
        out = jnp.where(sel == a, table[a:a + 1, :], out)
    return out


def _pair_candidates(v1, v2):
    sub = lax.broadcasted_iota(jnp.int32, (SUBLANES,) + v1.shape[1:], 0)
    cand, flat = [], []
    for b0 in (0, SUBLANES):
        cand.append(v1[0:1, :] + v2[b0:b0 + SUBLANES, :])
        flat.append(sub + b0)
    for a in range(1, SUBLANES):
        cand.append(v1[a:a + 1, :] + v2[0:SUBLANES, :])
        flat.append(sub + a * PEER_TOPK)
    cand.append(v1[SUBLANES:, :] + v2[0:1, :])
    flat.append((sub + SUBLANES) * PEER_TOPK)
    return jnp.concatenate(cand, axis=0), jnp.concatenate(flat, axis=0)


def _peer_topk_kernel(s_ref, e_ref, g_ref, et_ref, gt_ref):
    key_ids = lax.broadcasted_iota(jnp.int32, s_ref.shape[1:], 0)

    def head(hd):
        v1, i1 = _top16_rows(s_ref[2 * hd], key_ids, N_KEYS)
        v2, i2 = _top16_rows(s_ref[2 * hd + 1], key_ids, N_KEYS)
        cand, flat = _pair_candidates(v1, v2)
        cv, ci = _top16_rows(cand, flat, PEER_TOPK * PEER_TOPK)
        e1 = _select_rows(i1, ci // PEER_TOPK)
        e2 = _select_rows(i2, ci % PEER_TOPK)
        p = jnp.exp(cv - cv[0:1, :])
        gates = p / jnp.sum(p, axis=0, keepdims=True)
        rows = pl.ds(pl.multiple_of(hd * PEER_TOPK, PEER_TOPK), PEER_TOPK)
        et_ref[rows, :] = e1 * N_KEYS + e2
        gt_ref[rows, :] = gates

    def head_pair(j, carry):
        head(2 * j)
        head(2 * j + 1)
        return carry

    lax.fori_loop(0, PEER_HEADS // 2, head_pair, 0)
    e_ref[...] = et_ref[...].T
    g_ref[...] = gt_ref[...].T


def _peer_topk(s):
    ng, nk, t = s.shape
    tt = TOPK_TOKENS
    width = PEER_HEADS * PEER_TOPK
    return pl.pallas_call(
        _peer_topk_kernel,
        out_shape=(jax.ShapeDtypeStruct((t, width), jnp.int32), jax.ShapeDtypeStruct((t, width), F32)),
        grid=(t // tt,),
        in_specs=[pl.BlockSpec((ng, nk, tt), lambda i: (0, 0, i))],
        out_specs=(pl.BlockSpec((tt, width), lambda i: (i, 0)), pl.BlockSpec((tt, width), lambda i: (i, 0))),
        scratch_shapes=[pltpu.VMEM((width, tt), jnp.int32), pltpu.VMEM((width, tt), F32)],
        compiler_params=_cparams("arbitrary"),
        name="peer_topk",
    )(s)


PEER_PHASES = 4
PEER_PHASE_TOKENS = 4
PEER_TOKENS = PEER_PHASES * PEER_PHASE_TOKENS
PEER_PICKS = PEER_HEADS * PEER_TOPK


def _gelu_tanh(z):
    return 0.5 * z * (1.0 + jnp.tanh(math.sqrt(2.0 / math.pi) * (z + 0.044715 * (z * z * z))))


def _unpack_pair(w32):
    lo = lax.bitcast_convert_type(w32 << 16, F32)
    hi = lax.bitcast_convert_type(w32 & jnp.uint32(0xFFFF0000), F32)
    return lo, hi


def _sublane_sums(tiles):
    assert len(tiles) == SUBLANES
    sub = lax.broadcasted_iota(jnp.int32, tiles[0].shape, 0)
    dist = SUBLANES // 2
    while dist >= 1:
        keep_low = (sub & dist) == 0
        nxt = []
        for j in range(len(tiles) // 2):
            lo_t, hi_t = tiles[j], tiles[j + len(tiles) // 2]
            nxt.append(jnp.where(keep_low, lo_t + pltpu.roll(lo_t, SUBLANES - dist, 0),
                                 hi_t + pltpu.roll(hi_t, dist, 0)))
        tiles = nxt
        dist //= 2
    return tiles[0]


def _peer_mix_kernel(alpha, fuse_proj, *refs):
    if fuse_proj:
        (e_cur_ref, e_nxt_ref, gates_ref, x_ref, mod_ref, g_ref, b_ref, uv_ref, modn_ref, wn_ref,
         o_ref, proj_ref, buf0, buf1, buf2, buf3, h_ref, ffn_ref, sem_ref, wbuf_ref, hacc_ref, wsem) = refs
    else:
        (e_cur_ref, e_nxt_ref, gates_ref, x_ref, mod_ref, g_ref, b_ref, uv_ref,
         o_ref, buf0, buf1, buf2, buf3, h_ref, ffn_ref, sem_ref) = refs
    i = pl.program_id(0)
    n = pl.num_programs(0)
    d = x_ref.shape[1]
    bufs = (buf0, buf1, buf2, buf3)
    ptok = PEER_PHASE_TOKENS

    if fuse_proj:
        @pl.when(i == 0)
        def _():
            hacc_ref[...] = jnp.zeros_like(hacc_ref)
            cp = pltpu.make_async_copy(wn_ref, wbuf_ref, wsem)
            cp.start()
            cp.wait()

    def start_row(e_ref, row, buf, sem, t, k):
        pltpu.make_async_copy(uv_ref.at[e_ref[row, k]], buf.at[t, k], sem).start(priority=k % 2)

    def wait_buf(p):
        pltpu.make_async_copy(bufs[p], bufs[p], sem_ref.at[p]).wait()

    @pl.when(i == 0)
    def _():
        for p in range(2):
            for t in range(ptok):
                for k in range(PEER_PICKS):
                    start_row(e_cur_ref, p * ptok + t, bufs[p], sem_ref.at[p], t, k)

    x = x_ref[...]
    h = x * (1.0 + mod_ref[4:5, :]) + mod_ref[3:4, :]
    for s in range(d // LANES):
        h_ref[:, s, :] = h[:, s * LANES:(s + 1) * LANES]
    lane = lax.broadcasted_iota(jnp.int32, (SUBLANES, PEER_PICKS), 1)
    sub = lax.broadcasted_iota(jnp.int32, (SUBLANES, PEER_PICKS), 0)

    for p in range(PEER_PHASES):
        q = (p + 2) % PEER_PHASES
        ahead_ref, ahead_row0 = (e_cur_ref, (p + 2) * ptok) if p + 2 < PEER_PHASES else (e_nxt_ref, q * ptok)
        wait_buf(p)
        for t in range(ptok):
            r = p * ptok + t
            h_lo = h_ref[r, 0:SUBLANES, :]
            h_hi = h_ref[r, SUBLANES:, :]
            grow = gates_ref[r:r + 1, :]
            y_lo = jnp.zeros((SUBLANES, LANES), F32)
            y_hi = jnp.zeros((SUBLANES, LANES), F32)
            for g in range(PEER_PICKS // SUBLANES):
                k0 = g * SUBLANES
                prods = []
                for j in range(SUBLANES):
                    u_lo, u_hi = _unpack_pair(bufs[p][t, k0 + j, 0:SUBLANES, :])
                    start_row(ahead_ref, ahead_row0 + t, bufs[q], sem_ref.at[q], t, k0 + j)
                    prods.append(u_lo * h_lo + u_hi * h_hi)
                a = jnp.sum(_sublane_sums(prods), axis=1, keepdims=True)
                gate = jnp.sum(jnp.where(lane == sub + k0, grow, 0.0), axis=1, keepdims=True)
                w = jnp.broadcast_to(gate * _gelu_tanh(a), (SUBLANES, LANES))
                for j in range(SUBLANES):
                    v_lo, v_hi = _unpack_pair(bufs[p][t, k0 + j, SUBLANES:, :])
                    wj = jnp.broadcast_to(w[j:j + 1, :], (SUBLANES, LANES))
                    y_lo = y_lo + wj * v_lo
                    y_hi = y_hi + wj * v_hi
            ffn_ref[r, 0:SUBLANES, :] = y_lo
            ffn_ref[r, SUBLANES:, :] = y_hi

    @pl.when(i == n - 1)
    def _():
        wait_buf(0)
        wait_buf(1)

    ffn = jnp.concatenate([ffn_ref[:, s, :] for s in range(d // LANES)], axis=1)
    g2 = mod_ref[5:6, :]
    out = _layer_norm(alpha * x + (1.0 + g2) * ffn, g_ref[...], b_ref[...])
    o_ref[...] = out

    if fuse_proj:
        grp = i // PROJ_GROUP_STEPS
        pos = lax.rem(i, PROJ_GROUP_STEPS)
        hn = (out * (1.0 + modn_ref[1:2, :]) + modn_ref[0:1, :]).astype(BF16)
        rows = pl.ds(pl.multiple_of(pos * PEER_TOKENS, PEER_TOKENS), PEER_TOKENS)
        hacc_ref[lax.rem(grp, 2), rows, :] = hn
        c = jnp.minimum(pos, wbuf_ref.shape[0] - 1)
        proj_ref[...] = jnp.dot(hacc_ref[lax.rem(grp + 1, 2)], wbuf_ref[c], preferred_element_type=F32)


def _peer_mix(experts, gates, x2, mod_l, ln_g, ln_b, uvp, seq, alpha, mod_next=None, w_next=None):
    t, d = x2.shape
    tb = PEER_TOKENS
    nblk = t // tb
    per_b = seq // tb
    fuse_proj = w_next is not None
    const = lambda i: (0, 0)
    blk = lambda i: jnp.minimum(i, nblk - 1)
    in_specs = [
        pl.BlockSpec((tb, PEER_PICKS), lambda i: (blk(i), 0), memory_space=pltpu.SMEM),
        pl.BlockSpec((tb, PEER_PICKS), lambda i: (blk(i + 1), 0), memory_space=pltpu.SMEM),
        pl.BlockSpec((tb, PEER_PICKS), lambda i: (blk(i), 0)),
        pl.BlockSpec((tb, d), lambda i: (blk(i), 0)),
        pl.BlockSpec((None, 6, d), lambda i: (blk(i) // per_b, 0, 0)),
        pl.BlockSpec((1, d), const),
        pl.BlockSpec((1, d), const),
        pl.BlockSpec(memory_space=pl.ANY),
    ]
    out_shape = jax.ShapeDtypeStruct((t, d), F32)
    out_specs = pl.BlockSpec((tb, d), lambda i: (blk(i), 0))
    scratch = [pltpu.VMEM((PEER_PHASE_TOKENS, PEER_PICKS, d // LANES, LANES), jnp.uint32)] * PEER_PHASES + [
        pltpu.VMEM((tb, d // LANES, LANES), F32),
        pltpu.VMEM((tb, d // LANES, LANES), F32),
        pltpu.SemaphoreType.DMA((PEER_PHASES,))]
    operands = [experts, experts, gates, x2, mod_l, ln_g, ln_b, uvp]
    steps = nblk
    if fuse_proj:
        slices, _, pslice = w_next.shape
        assert slices == PROJ_GROUP_STEPS - 1 and nblk % PROJ_GROUP_STEPS == 0
        group_rows = PROJ_GROUP_STEPS * tb
        steps = nblk + PROJ_GROUP_STEPS
        in_specs += [pl.BlockSpec((None, 6, d), lambda i: (blk(i) // per_b, 0, 0)),
                     pl.BlockSpec(memory_space=pl.ANY)]
        operands += [mod_next, w_next]
        out_shape = (out_shape, jax.ShapeDtypeStruct((t, slices * pslice), F32))
        out_specs = (out_specs,
                     pl.BlockSpec((group_rows, pslice),
                                  lambda i: (jnp.maximum(i // PROJ_GROUP_STEPS - 1, 0),
                                             jnp.where(i < PROJ_GROUP_STEPS, 0,
                                                       jnp.minimum(i % PROJ_GROUP_STEPS, slices - 1)))))
        scratch += [pltpu.VMEM(w_next.shape, BF16),
                    pltpu.VMEM((2, group_rows, d), BF16),
                    pltpu.SemaphoreType.DMA]
    return pl.pallas_call(
        functools.partial(_peer_mix_kernel, alpha, fuse_proj),
        out_shape=out_shape,
        grid=(steps,),
        in_specs=in_specs,
        out_specs=out_specs,
        scratch_shapes=scratch,
        compiler_params=_cparams("arbitrary"),
        name="peer_mix",
    )(*operands)


PACK_ROWS = 256


def _bf16_bits_high(a):
    b = lax.bitcast_convert_type(a, jnp.uint32)
    return (b + jnp.uint32(0x7FFF) + ((b >> 16) & jnp.uint32(1))) & jnp.uint32(0xFFFF0000)


def _pack_table_kernel(u_ref, v_ref, o_ref):
    hd = u_ref.shape[1] // 2
    tiles = hd // LANES
    for part, src in enumerate((u_ref, v_ref)):
        words = (_bf16_bits_high(src[:, 0:hd]) >> 16) | _bf16_bits_high(src[:, hd:])
        for s in range(tiles):
            o_ref[:, part * tiles + s, :] = words[:, s * LANES:(s + 1) * LANES]


def _pack_table(peer_u, peer_v):
    depth, n, d = peer_u.shape
    spec = pl.BlockSpec((None, PACK_ROWS, d), lambda l, i: (l, i, 0))
    return pl.pallas_call(
        _pack_table_kernel,
        out_shape=jax.ShapeDtypeStruct((depth, n, d // LANES, LANES), jnp.uint32),
        grid=(depth, n // PACK_ROWS),
        in_specs=[spec, spec],
        out_specs=pl.BlockSpec((None, PACK_ROWS, d // LANES, LANES), lambda l, i: (l, i, 0, 0)),
        compiler_params=_cparams("arbitrary", "arbitrary"),
        name="pack_table",
    )(peer_u, peer_v)


def kernel(x, c, ada_w, ada_b, w_in, w_gate2, b_gate, gla_norm_g, conv_w, conv_norm_g, w_out, ln1_g, ln1_b,
           peer_wq, peer_keys, peer_u, peer_v, ln2_g, ln2_b):
    bsz, seq, d = x.shape
    depth = ada_w.shape[0]
    alpha = (2.0 * depth) ** 0.25
    t = bsz * seq

    mod = _ada_mod(c, ada_w, ada_b).reshape(depth, bsz, 6, d)

    qkvr = 2 * GLA_KEY_WIDTH + 2 * GLA_WIDTH
    w_alr = jnp.pad(w_in[:, :, qkvr:qkvr + GLA_GATE_RANK], ((0, 0), (0, 0), (0, ALR_PAD - GLA_GATE_RANK)))
    w_in_p = jnp.concatenate([w_in[:, :, :qkvr], w_in[:, :, qkvr + GLA_GATE_RANK:], w_alr], axis=-1).astype(BF16)
    wg_p = jnp.pad(w_gate2, ((0, 0), (0, ALR_PAD - GLA_GATE_RANK), (0, 0)))
    w_out_b = w_out.astype(BF16)
    wq_b = peer_wq.astype(BF16)
    keys_g = peer_keys.reshape(depth, 2 * PEER_HEADS, N_KEYS, PEER_HALF).astype(BF16)
    uv = _pack_table(peer_u, peer_v)

    w_in_s = w_in_p.reshape(depth, d, PROJ_WIDTH // PROJ_SLICE, PROJ_SLICE).transpose(0, 2, 1, 3)

    x2 = x.reshape(t, d)
    proj = _in_proj(x2, mod[0], w_in_p[0], seq)
    for l in range(depth):
        y = _gla_conv(proj, wg_p[l], b_gate[l][None], gla_norm_g[l][None], conv_w[l], conv_norm_g[l][None],
                      bsz, seq)
        x2 = _out_proj_ln(y, x2, mod[l], w_out_b[l], ln1_g[l][None], ln1_b[l][None], seq, alpha)
        s = _peer_scores(x2, mod[l], wq_b[l], keys_g[l], seq)
        experts, gates = _peer_topk(s)
        if l + 1 < depth:
            x2, proj = _peer_mix(experts, gates, x2, mod[l], ln2_g[l][None], ln2_b[l][None], uv[l], seq, alpha,
                                 mod_next=mod[l + 1], w_next=w_in_s[l + 1])
        else:
            x2 = _peer_mix(experts, gates, x2, mod[l], ln2_g[l][None], ln2_b[l][None], uv[l], seq, alpha)
    return x2.reshape(bsz, seq, d)
```

```python
import functools
import math

import jax
import jax.numpy as jnp
from jax import lax
from jax.experimental import pallas as pl
from jax.experimental.pallas import tpu as pltpu

CHUNK = 64
GLA_HEADS = 4
GLA_DK = 128
GLA_DV = 256
GLA_KEY_WIDTH = GLA_HEADS * GLA_DK
GLA_WIDTH = GLA_HEADS * GLA_DV
GLA_GATE_RANK = 16
GLA_GATE_TAU = 16.0
CONV_WIDTH = 1024
PEER_HEADS = 8
N_KEYS = 128
PEER_TOPK = 16
PEER_HALF = 128
EPS = 1e-5

LANES = 128
SUBLANES = 8
VMEM_LIMIT_BYTES = 56 * 1024 * 1024

ALR_PAD = LANES
PROJ_WIDTH = 2 * GLA_KEY_WIDTH + 2 * GLA_WIDTH + 3 * CONV_WIDTH + ALR_PAD

F32 = jnp.float32
BF16 = jnp.bfloat16
NEG_INF = float("-inf")


def _cparams(*sem):
    return pltpu.CompilerParams(dimension_semantics=sem, vmem_limit_bytes=VMEM_LIMIT_BYTES)


def _layer_norm(z, g, b):
    mu = jnp.mean(z, axis=-1, keepdims=True)
    zc = z - mu
    var = jnp.mean(zc * zc, axis=-1, keepdims=True)
    return zc * lax.rsqrt(var + EPS) * g + b


def _rms_norm(z, g):
    return z * lax.rsqrt(jnp.mean(z * z, axis=-1, keepdims=True) + EPS) * g


def _sigmoid(z):
    return 1.0 / (1.0 + jnp.exp(-z))


ADA_COLS = 1024


def _ada_kernel(c_ref, w_ref, b_ref, o_ref):
    c = c_ref[...]
    c_act = (c * _sigmoid(c)).astype(BF16)
    o_ref[...] = jnp.dot(c_act, w_ref[...].astype(BF16), preferred_element_type=F32) + b_ref[...]


def _ada_mod(c, ada_w, ada_b):
    depth, d, n = ada_w.shape
    bsz = c.shape[0]
    tn = ADA_COLS
    return pl.pallas_call(
        _ada_kernel,
        out_shape=jax.ShapeDtypeStruct((depth, bsz, n), F32),
        grid=(depth, n // tn),
        in_specs=[
            pl.BlockSpec((bsz, d), lambda l, j: (0, 0)),
            pl.BlockSpec((None, d, tn), lambda l, j: (l, 0, j)),
            pl.BlockSpec((None, 1, tn), lambda l, j: (l, 0, j)),
        ],
        out_specs=pl.BlockSpec((None, bsz, tn), lambda l, j: (l, 0, j)),
        compiler_params=_cparams("arbitrary", "arbitrary"),
        name="ada_mod",
    )(c, ada_w, ada_b.reshape(depth, 1, n))


IN_PROJ_ROWS = 1024
IN_PROJ_COLS = 896
ROW_TILE = 512


def _in_proj_kernel(x_ref, mod_ref, w_ref, o_ref, h_ref):
    @pl.when(pl.program_id(1) == 0)
    def _():
        sh = mod_ref[0:1, :]
        sc = mod_ref[1:2, :]
        h_ref[...] = (x_ref[...] * (1.0 + sc) + sh).astype(BF16)

    o_ref[...] = jnp.dot(h_ref[...], w_ref[...], preferred_element_type=F32)


def _in_proj(x2, mod_l, w_in_p, seq):
    t, d = x2.shape
    n = w_in_p.shape[1]
    tm, tn = min(IN_PROJ_ROWS, seq), IN_PROJ_COLS
    per_b = seq // tm
    return pl.pallas_call(
        _in_proj_kernel,
        out_shape=jax.ShapeDtypeStruct((t, n), F32),
        grid=(t // tm, n // tn),
        in_specs=[
            pl.BlockSpec((tm, d), lambda i, j: (i, 0)),
            pl.BlockSpec((None, 6, d), lambda i, j: (i // per_b, 0, 0)),
            pl.BlockSpec((d, tn), lambda i, j: (0, j)),
        ],
        out_specs=pl.BlockSpec((tm, tn), lambda i, j: (i, j)),
        scratch_shapes=[pltpu.VMEM((tm, d), BF16)],
        compiler_params=_cparams("arbitrary", "arbitrary"),
        name="in_proj",
    )(x2, mod_l, w_in_p)


GLA_ROWS = 256


def _gla_conv_kernel(q_ref, k_ref, v_ref, r_ref, cb_ref, cc_ref, ch_ref, alr_ref,
                     wg_ref, bg_ref, gng_ref, cw_ref, cng_ref, y_ref, state_ref, carry_ref):
    @pl.when(pl.program_id(1) == 0)
    def _():
        state_ref[...] = jnp.zeros_like(state_ref)
        carry_ref[...] = jnp.zeros_like(carry_ref)

    logit = jnp.dot(alr_ref[...], wg_ref[...], preferred_element_type=F32,
                    precision=lax.Precision.HIGHEST) + bg_ref[...]
    log_a = (jnp.minimum(logit, 0.0) - jnp.log(1.0 + jnp.exp(-jnp.abs(logit)))) * (1.0 / GLA_GATE_TAU)
    row = lax.broadcasted_iota(jnp.int32, (CHUNK, CHUNK), 0)
    col = lax.broadcasted_iota(jnp.int32, (CHUNK, CHUNK), 1)
    tri = (col <= row).astype(F32)
    gng = gng_ref[...]
    scale = GLA_DK ** -0.5
    for c in range(GLA_ROWS // CHUNK):
        rows = slice(c * CHUNK, (c + 1) * CHUNK)
        cum = jnp.dot(tri, log_a[rows], preferred_element_type=F32, precision=lax.Precision.HIGHEST)
        total = cum[CHUNK - 1:CHUNK, :]
        k_dec = (k_ref[rows, :] * jnp.exp(total - cum)).astype(BF16)
        dec = jnp.exp(total)
        qs = (q_ref[rows, :] * scale).astype(BF16)
        for h in range(GLA_HEADS):
            ks = slice(h * GLA_DK, (h + 1) * GLA_DK)
            vs = slice(h * GLA_DV, (h + 1) * GLA_DV)
            vh = v_ref[rows, vs].astype(BF16)
            upd = lax.dot_general(vh, k_dec[:, ks], (((0,), (0,)), ((), ())), preferred_element_type=F32)
            st = state_ref[h] * dec[:, ks] + upd
            state_ref[h] = st
            o = lax.dot_general(qs[:, ks], st.astype(BF16), (((1,), (1,)), ((), ())),
                                preferred_element_type=F32)
            rh = r_ref[rows, vs]
            y_ref[rows, vs] = (_rms_norm(o, gng) * (rh * _sigmoid(rh))).astype(y_ref.dtype)

    u = cc_ref[...] * ch_ref[...]
    prev = carry_ref[...]
    ridx = lax.broadcasted_iota(jnp.int32, (GLA_ROWS, 1), 0)
    u1 = jnp.where(ridx == 0, prev[SUBLANES - 1:SUBLANES, :], pltpu.roll(u, 1, 0))
    u2 = jnp.where(ridx == 0, prev[SUBLANES - 2:SUBLANES - 1, :],
                   jnp.where(ridx == 1, prev[SUBLANES - 1:SUBLANES, :], pltpu.roll(u, 2, 0)))
    conv = cw_ref[0:1, :] * u2 + cw_ref[1:2, :] * u1 + cw_ref[2:3, :] * u
    carry_ref[...] = u[GLA_ROWS - SUBLANES:, :]
    y_ref[:, GLA_WIDTH:] = _rms_norm(cb_ref[...] * conv, cng_ref[...]).astype(y_ref.dtype)


def _gla_conv(proj, wg_p, bg, gng, cw, cng, bsz, seq):
    t = proj.shape[0]
    nblk = seq // GLA_ROWS
    rmap = lambda width_blocks: (lambda b, j: (b * nblk + j, width_blocks))
    kw, vw = GLA_KEY_WIDTH, GLA_WIDTH
    const = lambda b, j: (0, 0)
    return pl.pallas_call(
        _gla_conv_kernel,
        out_shape=jax.ShapeDtypeStruct((t, GLA_WIDTH + CONV_WIDTH), BF16),
        grid=(bsz, nblk),
        in_specs=[
            pl.BlockSpec((GLA_ROWS, kw), rmap(0)),
            pl.BlockSpec((GLA_ROWS, kw), rmap(1)),
            pl.BlockSpec((GLA_ROWS, vw), rmap(1)),
            pl.BlockSpec((GLA_ROWS, vw), rmap(2)),
            pl.BlockSpec((GLA_ROWS, CONV_WIDTH), rmap(3)),
            pl.BlockSpec((GLA_ROWS, CONV_WIDTH), rmap(4)),
            pl.BlockSpec((GLA_ROWS, CONV_WIDTH), rmap(5)),
            pl.BlockSpec((GLA_ROWS, ALR_PAD), rmap((PROJ_WIDTH - ALR_PAD) // ALR_PAD)),
            pl.BlockSpec((ALR_PAD, kw), const),
            pl.BlockSpec((1, kw), const),
            pl.BlockSpec((1, GLA_DV), const),
            pl.BlockSpec((3, CONV_WIDTH), const),
            pl.BlockSpec((1, CONV_WIDTH), const),
        ],
        out_specs=pl.BlockSpec((GLA_ROWS, GLA_WIDTH + CONV_WIDTH), lambda b, j: (b * nblk + j, 0)),
        scratch_shapes=[pltpu.VMEM((GLA_HEADS, GLA_DV, GLA_DK), F32),
                        pltpu.VMEM((SUBLANES, CONV_WIDTH), F32)],
        compiler_params=_cparams("arbitrary", "arbitrary"),
        name="gla_conv",
    )(proj, proj, proj, proj, proj, proj, proj, proj, wg_p, bg, gng, cw, cng)


def _out_proj_ln_kernel(alpha, y_ref, x_ref, mod_ref, w_ref, g_ref, b_ref, o_ref):
    mix = jnp.dot(y_ref[...], w_ref[...], preferred_element_type=F32)
    g1 = mod_ref[2:3, :]
    o_ref[...] = _layer_norm(alpha * x_ref[...] + (1.0 + g1) * mix, g_ref[...], b_ref[...])


def _out_proj_ln(y, x2, mod_l, w_out, ln_g, ln_b, seq, alpha):
    t, d = x2.shape
    tm = ROW_TILE
    per_b = seq // tm
    const = lambda i: (0, 0)
    return pl.pallas_call(
        functools.partial(_out_proj_ln_kernel, alpha),
        out_shape=jax.ShapeDtypeStruct((t, d), F32),
        grid=(t // tm,),
        in_specs=[
            pl.BlockSpec((tm, y.shape[1]), lambda i: (i, 0)),
            pl.BlockSpec((tm, d), lambda i: (i, 0)),
            pl.BlockSpec((None, 6, d), lambda i: (i // per_b, 0, 0)),
            pl.BlockSpec(w_out.shape, const),
            pl.BlockSpec((1, d), const),
            pl.BlockSpec((1, d), const),
        ],
        out_specs=pl.BlockSpec((tm, d), lambda i: (i, 0)),
        compiler_params=_cparams("arbitrary"),
        name="out_proj_ln",
    )(y, x2, mod_l, w_out, ln_g, ln_b)


def _peer_scores_kernel(x_ref, mod_ref, wq_ref, keys_ref, s_ref):
    sh = mod_ref[3:4, :]
    sc = mod_ref[4:5, :]
    h = (x_ref[...] * (1.0 + sc) + sh).astype(BF16)
    q = jnp.dot(h, wq_ref[...], preferred_element_type=F32).astype(BF16)
    for g in range(2 * PEER_HEADS):
        qg = q[:, g * PEER_HALF:(g + 1) * PEER_HALF]
        s_ref[g] = lax.dot_general(keys_ref[g], qg, (((1,), (1,)), ((), ())), preferred_element_type=F32)


def _peer_scores(x2, mod_l, wq, keys_g, seq):
    t, d = x2.shape
    tm = ROW_TILE
    per_b = seq // tm
    ng = 2 * PEER_HEADS
    return pl.pallas_call(
        _peer_scores_kernel,
        out_shape=jax.ShapeDtypeStruct((ng, N_KEYS, t), F32),
        grid=(t // tm,),
        in_specs=[
            pl.BlockSpec((tm, d), lambda i: (i, 0)),
            pl.BlockSpec((None, 6, d), lambda i: (i // per_b, 0, 0)),
            pl.BlockSpec(wq.shape, lambda i: (0, 0)),
            pl.BlockSpec(keys_g.shape, lambda i: (0, 0, 0)),
        ],
        out_specs=pl.BlockSpec((ng, N_KEYS, tm), lambda i: (0, 0, i)),
        compiler_params=_cparams("arbitrary"),
        name="peer_scores",
    )(x2, mod_l, wq, keys_g)


TOPK_TOKENS = LANES


def _top16_rows(s, ids, big):
    vals, out_ids = [], []
    for _ in range(PEER_TOPK):
        m = jnp.max(s, axis=0, keepdims=True)
        idx = jnp.min(jnp.where(s == m, ids, big), axis=0, keepdims=True)
        vals.append(m)
        out_ids.append(idx)
        s = jnp.where(ids == idx, NEG_INF, s)
    return jnp.concatenate(vals, axis=0), jnp.concatenate(out_ids, axis=0)


def _select_rows(table, sel):
    out = jnp.zeros(sel.shape, table.dtype)
    for a in range(PEER_TOPK):
        out = jnp.where(sel == a, table[a:a + 1, :], out)
    return out


def _pair_candidates(v1, v2):
    sub = lax.broadcasted_iota(jnp.int32, (SUBLANES,) + v1.shape[1:], 0)
    cand, flat = [], []
    for b0 in (0, SUBLANES):
        cand.append(v1[0:1, :] + v2[b0:b0 + SUBLANES, :])
        flat.append(sub + b0)
    for a in range(1, SUBLANES):
        cand.append(v1[a:a + 1, :] + v2[0:SUBLANES, :])
        flat.append(sub + a * PEER_TOPK)
    cand.append(v1[SUBLANES:, :] + v2[0:1, :])
    flat.append((sub + SUBLANES) * PEER_TOPK)
    return jnp.concatenate(cand, axis=0), jnp.concatenate(flat, axis=0)


def _peer_topk_kernel(s_ref, e_ref, g_ref, et_ref, gt_ref):
    key_ids = lax.broadcasted_iota(jnp.int32, s_ref.shape[1:], 0)

    def head(hd):
        v1, i1 = _top16_rows(s_ref[2 * hd], key_ids, N_KEYS)
        v2, i2 = _top16_rows(s_ref[2 * hd + 1], key_ids, N_KEYS)
        cand, flat = _pair_candidates(v1, v2)
        cv, ci = _top16_rows(cand, flat, PEER_TOPK * PEER_TOPK)
        e1 = _select_rows(i1, ci // PEER_TOPK)
        e2 = _select_rows(i2, ci % PEER_TOPK)
        p = jnp.exp(cv - cv[0:1, :])
        gates = p / jnp.sum(p, axis=0, keepdims=True)
        rows = pl.ds(pl.multiple_of(hd * PEER_TOPK, PEER_TOPK), PEER_TOPK)
        et_ref[rows, :] = e1 * N_KEYS + e2
        gt_ref[rows, :] = gates

    def head_pair(j, carry):
        head(2 * j)
        head(2 * j + 1)
        return carry

    lax.fori_loop(0, PEER_HEADS // 2, head_pair, 0)
    e_ref[...] = et_ref[...].T
    g_ref[...] = gt_ref[...].T


def _peer_topk(s):
    ng, nk, t = s.shape
    tt = TOPK_TOKENS
    width = PEER_HEADS * PEER_TOPK
    return pl.pallas_call(
        _peer_topk_kernel,
        out_shape=(jax.ShapeDtypeStruct((t, width), jnp.int32), jax.ShapeDtypeStruct((t, width), F32)),
        grid=(t // tt,),
        in_specs=[pl.BlockSpec((ng, nk, tt), lambda i: (0, 0, i))],
        out_specs=(pl.BlockSpec((tt, width), lambda i: (i, 0)), pl.BlockSpec((tt, width), lambda i: (i, 0))),
        scratch_shapes=[pltpu.VMEM((width, tt), jnp.int32), pltpu.VMEM((width, tt), F32)],
        compiler_params=_cparams("arbitrary"),
        name="peer_topk",
    )(s)


PEER_PHASES = 4
PEER_PHASE_TOKENS = 4
PEER_TOKENS = PEER_PHASES * PEER_PHASE_TOKENS
PEER_PICKS = PEER_HEADS * PEER_TOPK


def _gelu_tanh(z):
    return 0.5 * z * (1.0 + jnp.tanh(math.sqrt(2.0 / math.pi) * (z + 0.044715 * (z * z * z))))


def _unpack_pair(w32):
    lo = lax.bitcast_convert_type(w32 << 16, F32)
    hi = lax.bitcast_convert_type(w32 & jnp.uint32(0xFFFF0000), F32)
    return lo, hi


def _sublane_sums(tiles):
    assert len(tiles) == SUBLANES
    sub = lax.broadcasted_iota(jnp.int32, tiles[0].shape, 0)
    dist = SUBLANES // 2
    while dist >= 1:
        keep_low = (sub & dist) == 0
        nxt = []
        for j in range(len(tiles) // 2):
            lo_t, hi_t = tiles[j], tiles[j + len(tiles) // 2]
            nxt.append(jnp.where(keep_low, lo_t + pltpu.roll(lo_t, SUBLANES - dist, 0),
                                 hi_t + pltpu.roll(hi_t, dist, 0)))
        tiles = nxt
        dist //= 2
    return tiles[0]


def _peer_mix_kernel(alpha, e_cur_ref, e_nxt_ref, gates_ref, x_ref, mod_ref, g_ref, b_ref, uv_ref,
                     o_ref, buf0, buf1, buf2, buf3, h_ref, ffn_ref, sem_ref):
    i = pl.program_id(0)
    n = pl.num_programs(0)
    d = x_ref.shape[1]
    bufs = (buf0, buf1, buf2, buf3)
    ptok = PEER_PHASE_TOKENS

    def start_row(e_ref, row, buf, sem, t, k):
        pltpu.make_async_copy(uv_ref.at[e_ref[row, k]], buf.at[t, k], sem).start(priority=k % 2)

    def wait_buf(p):
        pltpu.make_async_copy(bufs[p], bufs[p], sem_ref.at[p]).wait()

    @pl.when(i == 0)
    def _():
        for p in range(2):
            for t in range(ptok):
                for k in range(PEER_PICKS):
                    start_row(e_cur_ref, p * ptok + t, bufs[p], sem_ref.at[p], t, k)

    x = x_ref[...]
    h = x * (1.0 + mod_ref[4:5, :]) + mod_ref[3:4, :]
    for s in range(d // LANES):
        h_ref[:, s, :] = h[:, s * LANES:(s + 1) * LANES]
    lane = lax.broadcasted_iota(jnp.int32, (SUBLANES, PEER_PICKS), 1)
    sub = lax.broadcasted_iota(jnp.int32, (SUBLANES, PEER_PICKS), 0)

    for p in range(PEER_PHASES):
        q = (p + 2) % PEER_PHASES
        ahead_ref, ahead_row0 = (e_cur_ref, (p + 2) * ptok) if p + 2 < PEER_PHASES else (e_nxt_ref, q * ptok)
        wait_buf(p)
        for t in range(ptok):
            r = p * ptok + t
            h_lo = h_ref[r, 0:SUBLANES, :]
            h_hi = h_ref[r, SUBLANES:, :]
            grow = gates_ref[r:r + 1, :]
            y_lo = jnp.zeros((SUBLANES, LANES), F32)
            y_hi = jnp.zeros((SUBLANES, LANES), F32)
            for g in range(PEER_PICKS // SUBLANES):
                k0 = g * SUBLANES
                prods = []
                for j in range(SUBLANES):
                    u_lo, u_hi = _unpack_pair(bufs[p][t, k0 + j, 0:SUBLANES, :])
                    start_row(ahead_ref, ahead_row0 + t, bufs[q], sem_ref.at[q], t, k0 + j)
                    prods.append(u_lo * h_lo + u_hi * h_hi)
                a = jnp.sum(_sublane_sums(prods), axis=1, keepdims=True)
                gate = jnp.sum(jnp.where(lane == sub + k0, grow, 0.0), axis=1, keepdims=True)
                w = jnp.broadcast_to(gate * _gelu_tanh(a), (SUBLANES, LANES))
                for j in range(SUBLANES):
                    v_lo, v_hi = _unpack_pair(bufs[p][t, k0 + j, SUBLANES:, :])
                    wj = jnp.broadcast_to(w[j:j + 1, :], (SUBLANES, LANES))
                    y_lo = y_lo + wj * v_lo
                    y_hi = y_hi + wj * v_hi
            ffn_ref[r, 0:SUBLANES, :] = y_lo
            ffn_ref[r, SUBLANES:, :] = y_hi

    @pl.when(i == n - 1)
    def _():
        wait_buf(0)
        wait_buf(1)

    ffn = jnp.concatenate([ffn_ref[:, s, :] for s in range(d // LANES)], axis=1)
    g2 = mod_ref[5:6, :]
    o_ref[...] = _layer_norm(alpha * x + (1.0 + g2) * ffn, g_ref[...], b_ref[...])


def _peer_mix(experts, gates, x2, mod_l, ln_g, ln_b, uvp, seq, alpha):
    t, d = x2.shape
    tb = PEER_TOKENS
    nblk = t // tb
    per_b = seq // tb
    const = lambda i: (0, 0)
    return pl.pallas_call(
        functools.partial(_peer_mix_kernel, alpha),
        out_shape=jax.ShapeDtypeStruct((t, d), F32),
        grid=(nblk,),
        in_specs=[
            pl.BlockSpec((tb, PEER_PICKS), lambda i: (i, 0), memory_space=pltpu.SMEM),
            pl.BlockSpec((tb, PEER_PICKS), lambda i: (jnp.minimum(i + 1, nblk - 1), 0),
                         memory_space=pltpu.SMEM),
            pl.BlockSpec((tb, PEER_PICKS), lambda i: (i, 0)),
            pl.BlockSpec((tb, d), lambda i: (i, 0)),
            pl.BlockSpec((None, 6, d), lambda i: (i // per_b, 0, 0)),
            pl.BlockSpec((1, d), const),
            pl.BlockSpec((1, d), const),
            pl.BlockSpec(memory_space=pl.ANY),
        ],
        out_specs=pl.BlockSpec((tb, d), lambda i: (i, 0)),
        scratch_shapes=[pltpu.VMEM((PEER_PHASE_TOKENS, PEER_PICKS, d // LANES, LANES), jnp.uint32)] * PEER_PHASES + [
                        pltpu.VMEM((tb, d // LANES, LANES), F32),
                        pltpu.VMEM((tb, d // LANES, LANES), F32),
                        pltpu.SemaphoreType.DMA((PEER_PHASES,))],
        compiler_params=_cparams("arbitrary"),
        name="peer_mix",
    )(experts, experts, gates, x2, mod_l, ln_g, ln_b, uvp)


PACK_ROWS = 512


def _bf16_bits_high(a):
    b = lax.bitcast_convert_type(a, jnp.uint32)
    return (b + jnp.uint32(0x7FFF) + ((b >> 16) & jnp.uint32(1))) & jnp.uint32(0xFFFF0000)


def _pack_table_kernel(u_ref, v_ref, o_ref):
    hd = u_ref.shape[1] // 2
    tiles = hd // LANES
    for part, src in enumerate((u_ref, v_ref)):
        words = (_bf16_bits_high(src[:, 0:hd]) >> 16) | _bf16_bits_high(src[:, hd:])
        for s in range(tiles):
            o_ref[:, part * tiles + s, :] = words[:, s * LANES:(s + 1) * LANES]


def _pack_table(peer_u, peer_v):
    depth, n, d = peer_u.shape
    spec = pl.BlockSpec((None, PACK_ROWS, d), lambda l, i: (l, i, 0))
    return pl.pallas_call(
        _pack_table_kernel,
        out_shape=jax.ShapeDtypeStruct((depth, n, d // LANES, LANES), jnp.uint32),
        grid=(depth, n // PACK_ROWS),
        in_specs=[spec, spec],
        out_specs=pl.BlockSpec((None, PACK_ROWS, d // LANES, LANES), lambda l, i: (l, i, 0, 0)),
        compiler_params=_cparams("arbitrary", "arbitrary"),
        name="pack_table",
    )(peer_u, peer_v)


def kernel(x, c, ada_w, ada_b, w_in, w_gate2, b_gate, gla_norm_g, conv_w, conv_norm_g, w_out, ln1_g, ln1_b,
           peer_wq, peer_keys, peer_u, peer_v, ln2_g, ln2_b):
    bsz, seq, d = x.shape
    depth = ada_w.shape[0]
    alpha = (2.0 * depth) ** 0.25
    t = bsz * seq

    mod = _ada_mod(c, ada_w, ada_b).reshape(depth, bsz, 6, d)

    qkvr = 2 * GLA_KEY_WIDTH + 2 * GLA_WIDTH
    w_alr = jnp.pad(w_in[:, :, qkvr:qkvr + GLA_GATE_RANK], ((0, 0), (0, 0), (0, ALR_PAD - GLA_GATE_RANK)))
    w_in_p = jnp.concatenate([w_in[:, :, :qkvr], w_in[:, :, qkvr + GLA_GATE_RANK:], w_alr], axis=-1).astype(BF16)
    wg_p = jnp.pad(w_gate2, ((0, 0), (0, ALR_PAD - GLA_GATE_RANK), (0, 0)))
    w_out_b = w_out.astype(BF16)
    wq_b = peer_wq.astype(BF16)
    keys_g = peer_keys.reshape(depth, 2 * PEER_HEADS, N_KEYS, PEER_HALF).astype(BF16)
    uv = _pack_table(peer_u, peer_v)

    x2 = x.reshape(t, d)
    for l in range(depth):
        proj = _in_proj(x2, mod[l], w_in_p[l], seq)
        y = _gla_conv(proj, wg_p[l], b_gate[l][None], gla_norm_g[l][None], conv_w[l], conv_norm_g[l][None],
                      bsz, seq)
        x2 = _out_proj_ln(y, x2, mod[l], w_out_b[l], ln1_g[l][None], ln1_b[l][None], seq, alpha)
        s = _peer_scores(x2, mod[l], wq_b[l], keys_g[l], seq)
        experts, gates = _peer_topk(s)
        x2 = _peer_mix(experts, gates, x2, mod[l], ln2_g[l][None], ln2_b[l][None], uv[l], seq, alpha)
    return x2.reshape(bsz, seq, d)
```

```python
import functools
import math

import jax
import jax.numpy as jnp
from jax import lax
from jax.experimental import pallas as pl
from jax.experimental.pallas import tpu as pltpu

CHUNK = 64
GLA_HEADS = 4
GLA_DK = 128
GLA_DV = 256
GLA_KEY_WIDTH = GLA_HEADS * GLA_DK
GLA_WIDTH = GLA_HEADS * GLA_DV
GLA_GATE_RANK = 16
GLA_GATE_TAU = 16.0
CONV_WIDTH = 1024
PEER_HEADS = 8
N_KEYS = 128
PEER_TOPK = 16
PEER_HALF = 128
EPS = 1e-5

LANES = 128
SUBLANES = 8
VMEM_LIMIT_BYTES = 56 * 1024 * 1024

ALR_PAD = LANES
PROJ_WIDTH = 2 * GLA_KEY_WIDTH + 2 * GLA_WIDTH + 3 * CONV_WIDTH + ALR_PAD

F32 = jnp.float32
BF16 = jnp.bfloat16
NEG_INF = float("-inf")


def _cparams(*sem):
    return pltpu.CompilerParams(dimension_semantics=sem, vmem_limit_bytes=VMEM_LIMIT_BYTES)


def _layer_norm(z, g, b):
    mu = jnp.mean(z, axis=-1, keepdims=True)
    zc = z - mu
    var = jnp.mean(zc * zc, axis=-1, keepdims=True)
    return zc * lax.rsqrt(var + EPS) * g + b


def _rms_norm(z, g):
    return z * lax.rsqrt(jnp.mean(z * z, axis=-1, keepdims=True) + EPS) * g


def _sigmoid(z):
    return 1.0 / (1.0 + jnp.exp(-z))


ADA_COLS = 1024


def _ada_kernel(c_ref, w_ref, b_ref, o_ref):
    c = c_ref[...]
    c_act = (c * _sigmoid(c)).astype(BF16)
    o_ref[...] = jnp.dot(c_act, w_ref[...].astype(BF16), preferred_element_type=F32) + b_ref[...]


def _ada_mod(c, ada_w, ada_b):
    depth, d, n = ada_w.shape
    bsz = c.shape[0]
    tn = ADA_COLS
    return pl.pallas_call(
        _ada_kernel,
        out_shape=jax.ShapeDtypeStruct((depth, bsz, n), F32),
        grid=(depth, n // tn),
        in_specs=[
            pl.BlockSpec((bsz, d), lambda l, j: (0, 0)),
            pl.BlockSpec((None, d, tn), lambda l, j: (l, 0, j)),
            pl.BlockSpec((None, 1, tn), lambda l, j: (l, 0, j)),
        ],
        out_specs=pl.BlockSpec((None, bsz, tn), lambda l, j: (l, 0, j)),
        compiler_params=_cparams("arbitrary", "arbitrary"),
        name="ada_mod",
    )(c, ada_w, ada_b.reshape(depth, 1, n))


IN_PROJ_ROWS = 1024
IN_PROJ_COLS = 896
ROW_TILE = 512


def _in_proj_kernel(x_ref, mod_ref, w_ref, o_ref, h_ref):
    @pl.when(pl.program_id(1) == 0)
    def _():
        sh = mod_ref[0:1, :]
        sc = mod_ref[1:2, :]
        h_ref[...] = (x_ref[...] * (1.0 + sc) + sh).astype(BF16)

    o_ref[...] = jnp.dot(h_ref[...], w_ref[...], preferred_element_type=F32)


def _in_proj(x2, mod_l, w_in_p, seq):
    t, d = x2.shape
    n = w_in_p.shape[1]
    tm, tn = min(IN_PROJ_ROWS, seq), IN_PROJ_COLS
    per_b = seq // tm
    return pl.pallas_call(
        _in_proj_kernel,
        out_shape=jax.ShapeDtypeStruct((t, n), F32),
        grid=(t // tm, n // tn),
        in_specs=[
            pl.BlockSpec((tm, d), lambda i, j: (i, 0)),
            pl.BlockSpec((None, 6, d), lambda i, j: (i // per_b, 0, 0)),
            pl.BlockSpec((d, tn), lambda i, j: (0, j)),
        ],
        out_specs=pl.BlockSpec((tm, tn), lambda i, j: (i, j)),
        scratch_shapes=[pltpu.VMEM((tm, d), BF16)],
        compiler_params=_cparams("arbitrary", "arbitrary"),
        name="in_proj",
    )(x2, mod_l, w_in_p)


GLA_ROWS = 256


def _gla_conv_kernel(q_ref, k_ref, v_ref, r_ref, cb_ref, cc_ref, ch_ref, alr_ref,
                     wg_ref, bg_ref, gng_ref, cw_ref, cng_ref, y_ref, state_ref, carry_ref):
    @pl.when(pl.program_id(1) == 0)
    def _():
        state_ref[...] = jnp.zeros_like(state_ref)
        carry_ref[...] = jnp.zeros_like(carry_ref)

    logit = jnp.dot(alr_ref[...], wg_ref[...], preferred_element_type=F32,
                    precision=lax.Precision.HIGHEST) + bg_ref[...]
    log_a = (jnp.minimum(logit, 0.0) - jnp.log(1.0 + jnp.exp(-jnp.abs(logit)))) * (1.0 / GLA_GATE_TAU)
    row = lax.broadcasted_iota(jnp.int32, (CHUNK, CHUNK), 0)
    col = lax.broadcasted_iota(jnp.int32, (CHUNK, CHUNK), 1)
    tri = (col <= row).astype(F32)
    gng = gng_ref[...]
    scale = GLA_DK ** -0.5
    for c in range(GLA_ROWS // CHUNK):
        rows = slice(c * CHUNK, (c + 1) * CHUNK)
        cum = jnp.dot(tri, log_a[rows], preferred_element_type=F32, precision=lax.Precision.HIGHEST)
        total = cum[CHUNK - 1:CHUNK, :]
        k_dec = (k_ref[rows, :] * jnp.exp(total - cum)).astype(BF16)
        dec = jnp.exp(total)
        qs = (q_ref[rows, :] * scale).astype(BF16)
        for h in range(GLA_HEADS):
            ks = slice(h * GLA_DK, (h + 1) * GLA_DK)
            vs = slice(h * GLA_DV, (h + 1) * GLA_DV)
            vh = v_ref[rows, vs].astype(BF16)
            upd = lax.dot_general(vh, k_dec[:, ks], (((0,), (0,)), ((), ())), preferred_element_type=F32)
            st = state_ref[h] * dec[:, ks] + upd
            state_ref[h] = st
            o = lax.dot_general(qs[:, ks], st.astype(BF16), (((1,), (1,)), ((), ())),
                                preferred_element_type=F32)
            rh = r_ref[rows, vs]
            y_ref[rows, vs] = (_rms_norm(o, gng) * (rh * _sigmoid(rh))).astype(y_ref.dtype)

    u = cc_ref[...] * ch_ref[...]
    prev = carry_ref[...]
    ridx = lax.broadcasted_iota(jnp.int32, (GLA_ROWS, 1), 0)
    u1 = jnp.where(ridx == 0, prev[SUBLANES - 1:SUBLANES, :], pltpu.roll(u, 1, 0))
    u2 = jnp.where(ridx == 0, prev[SUBLANES - 2:SUBLANES - 1, :],
                   jnp.where(ridx == 1, prev[SUBLANES - 1:SUBLANES, :], pltpu.roll(u, 2, 0)))
    conv = cw_ref[0:1, :] * u2 + cw_ref[1:2, :] * u1 + cw_ref[2:3, :] * u
    carry_ref[...] = u[GLA_ROWS - SUBLANES:, :]
    y_ref[:, GLA_WIDTH:] = _rms_norm(cb_ref[...] * conv, cng_ref[...]).astype(y_ref.dtype)


def _gla_conv(proj, wg_p, bg, gng, cw, cng, bsz, seq):
    t = proj.shape[0]
    nblk = seq // GLA_ROWS
    rmap = lambda width_blocks: (lambda b, j: (b * nblk + j, width_blocks))
    kw, vw = GLA_KEY_WIDTH, GLA_WIDTH
    const = lambda b, j: (0, 0)
    return pl.pallas_call(
        _gla_conv_kernel,
        out_shape=jax.ShapeDtypeStruct((t, GLA_WIDTH + CONV_WIDTH), BF16),
        grid=(bsz, nblk),
        in_specs=[
            pl.BlockSpec((GLA_ROWS, kw), rmap(0)),
            pl.BlockSpec((GLA_ROWS, kw), rmap(1)),
            pl.BlockSpec((GLA_ROWS, vw), rmap(1)),
            pl.BlockSpec((GLA_ROWS, vw), rmap(2)),
            pl.BlockSpec((GLA_ROWS, CONV_WIDTH), rmap(3)),
            pl.BlockSpec((GLA_ROWS, CONV_WIDTH), rmap(4)),
            pl.BlockSpec((GLA_ROWS, CONV_WIDTH), rmap(5)),
            pl.BlockSpec((GLA_ROWS, ALR_PAD), rmap((PROJ_WIDTH - ALR_PAD) // ALR_PAD)),
            pl.BlockSpec((ALR_PAD, kw), const),
            pl.BlockSpec((1, kw), const),
            pl.BlockSpec((1, GLA_DV), const),
            pl.BlockSpec((3, CONV_WIDTH), const),
            pl.BlockSpec((1, CONV_WIDTH), const),
        ],
        out_specs=pl.BlockSpec((GLA_ROWS, GLA_WIDTH + CONV_WIDTH), lambda b, j: (b * nblk + j, 0)),
        scratch_shapes=[pltpu.VMEM((GLA_HEADS, GLA_DV, GLA_DK), F32),
                        pltpu.VMEM((SUBLANES, CONV_WIDTH), F32)],
        compiler_params=_cparams("arbitrary", "arbitrary"),
        name="gla_conv",
    )(proj, proj, proj, proj, proj, proj, proj, proj, wg_p, bg, gng, cw, cng)


def _out_proj_ln_kernel(alpha, y_ref, x_ref, mod_ref, w_ref, g_ref, b_ref, o_ref):
    mix = jnp.dot(y_ref[...], w_ref[...], preferred_element_type=F32)
    g1 = mod_ref[2:3, :]
    o_ref[...] = _layer_norm(alpha * x_ref[...] + (1.0 + g1) * mix, g_ref[...], b_ref[...])


def _out_proj_ln(y, x2, mod_l, w_out, ln_g, ln_b, seq, alpha):
    t, d = x2.shape
    tm = ROW_TILE
    per_b = seq // tm
    const = lambda i: (0, 0)
    return pl.pallas_call(
        functools.partial(_out_proj_ln_kernel, alpha),
        out_shape=jax.ShapeDtypeStruct((t, d), F32),
        grid=(t // tm,),
        in_specs=[
            pl.BlockSpec((tm, y.shape[1]), lambda i: (i, 0)),
            pl.BlockSpec((tm, d), lambda i: (i, 0)),
            pl.BlockSpec((None, 6, d), lambda i: (i // per_b, 0, 0)),
            pl.BlockSpec(w_out.shape, const),
            pl.BlockSpec((1, d), const),
            pl.BlockSpec((1, d), const),
        ],
        out_specs=pl.BlockSpec((tm, d), lambda i: (i, 0)),
        compiler_params=_cparams("arbitrary"),
        name="out_proj_ln",
    )(y, x2, mod_l, w_out, ln_g, ln_b)


def _peer_scores_kernel(x_ref, mod_ref, wq_ref, keys_ref, s_ref):
    sh = mod_ref[3:4, :]
    sc = mod_ref[4:5, :]
    h = (x_ref[...] * (1.0 + sc) + sh).astype(BF16)
    q = jnp.dot(h, wq_ref[...], preferred_element_type=F32).astype(BF16)
    for g in range(2 * PEER_HEADS):
        qg = q[:, g * PEER_HALF:(g + 1) * PEER_HALF]
        s_ref[g] = lax.dot_general(keys_ref[g], qg, (((1,), (1,)), ((), ())), preferred_element_type=F32)


def _peer_scores(x2, mod_l, wq, keys_g, seq):
    t, d = x2.shape
    tm = ROW_TILE
    per_b = seq // tm
    ng = 2 * PEER_HEADS
    return pl.pallas_call(
        _peer_scores_kernel,
        out_shape=jax.ShapeDtypeStruct((ng, N_KEYS, t), F32),
        grid=(t // tm,),
        in_specs=[
            pl.BlockSpec((tm, d), lambda i: (i, 0)),
            pl.BlockSpec((None, 6, d), lambda i: (i // per_b, 0, 0)),
            pl.BlockSpec(wq.shape, lambda i: (0, 0)),
            pl.BlockSpec(keys_g.shape, lambda i: (0, 0, 0)),
        ],
        out_specs=pl.BlockSpec((ng, N_KEYS, tm), lambda i: (0, 0, i)),
        compiler_params=_cparams("arbitrary"),
        name="peer_scores",
    )(x2, mod_l, wq, keys_g)


TOPK_TOKENS = LANES


def _top16_rows(s, ids, big):
    vals, out_ids = [], []
    for _ in range(PEER_TOPK):
        m = jnp.max(s, axis=0, keepdims=True)
        idx = jnp.min(jnp.where(s == m, ids, big), axis=0, keepdims=True)
        vals.append(m)
        out_ids.append(idx)
        s = jnp.where(ids == idx, NEG_INF, s)
    return jnp.concatenate(vals, axis=0), jnp.concatenate(out_ids, axis=0)


def _select_rows(table, sel):
    out = jnp.zeros(sel.shape, table.dtype)
    for a in range(PEER_TOPK):
        out = jnp.where(sel == a, table[a:a + 1, :], out)
    return out


def _pair_candidates(v1, v2):
    sub = lax.broadcasted_iota(jnp.int32, (SUBLANES,) + v1.shape[1:], 0)
    cand, flat = [], []
    for b0 in (0, SUBLANES):
        cand.append(v1[0:1, :] + v2[b0:b0 + SUBLANES, :])
        flat.append(sub + b0)
    for a in range(1, SUBLANES):
        cand.append(v1[a:a + 1, :] + v2[0:SUBLANES, :])
        flat.append(sub + a * PEER_TOPK)
    cand.append(v1[SUBLANES:, :] + v2[0:1, :])
    flat.append((sub + SUBLANES) * PEER_TOPK)
    return jnp.concatenate(cand, axis=0), jnp.concatenate(flat, axis=0)


def _peer_topk_kernel(s_ref, e_ref, g_ref, et_ref, gt_ref):
    key_ids = lax.broadcasted_iota(jnp.int32, s_ref.shape[1:], 0)

    def head(hd):
        v1, i1 = _top16_rows(s_ref[2 * hd], key_ids, N_KEYS)
        v2, i2 = _top16_rows(s_ref[2 * hd + 1], key_ids, N_KEYS)
        cand, flat = _pair_candidates(v1, v2)
        cv, ci = _top16_rows(cand, flat, PEER_TOPK * PEER_TOPK)
        e1 = _select_rows(i1, ci // PEER_TOPK)
        e2 = _select_rows(i2, ci % PEER_TOPK)
        p = jnp.exp(cv - cv[0:1, :])
        gates = p / jnp.sum(p, axis=0, keepdims=True)
        rows = pl.ds(pl.multiple_of(hd * PEER_TOPK, PEER_TOPK), PEER_TOPK)
        et_ref[rows, :] = e1 * N_KEYS + e2
        gt_ref[rows, :] = gates

    for hd in range(PEER_HEADS):
        head(hd)
    e_ref[...] = et_ref[...].T
    g_ref[...] = gt_ref[...].T


def _peer_topk(s):
    ng, nk, t = s.shape
    tt = TOPK_TOKENS
    width = PEER_HEADS * PEER_TOPK
    return pl.pallas_call(
        _peer_topk_kernel,
        out_shape=(jax.ShapeDtypeStruct((t, width), jnp.int32), jax.ShapeDtypeStruct((t, width), F32)),
        grid=(t // tt,),
        in_specs=[pl.BlockSpec((ng, nk, tt), lambda i: (0, 0, i))],
        out_specs=(pl.BlockSpec((tt, width), lambda i: (i, 0)), pl.BlockSpec((tt, width), lambda i: (i, 0))),
        scratch_shapes=[pltpu.VMEM((width, tt), jnp.int32), pltpu.VMEM((width, tt), F32)],
        compiler_params=_cparams("arbitrary"),
        name="peer_topk",
    )(s)


PEER_PHASES = 4
PEER_PHASE_TOKENS = 4
PEER_TOKENS = PEER_PHASES * PEER_PHASE_TOKENS
PEER_PICKS = PEER_HEADS * PEER_TOPK


def _gelu_tanh(z):
    return 0.5 * z * (1.0 + jnp.tanh(math.sqrt(2.0 / math.pi) * (z + 0.044715 * (z * z * z))))


def _unpack_pair(w32):
    lo = lax.bitcast_convert_type(w32 << 16, F32)
    hi = lax.bitcast_convert_type(w32 & jnp.uint32(0xFFFF0000), F32)
    return lo, hi


def _sublane_sums(tiles):
    assert len(tiles) == SUBLANES
    sub = lax.broadcasted_iota(jnp.int32, tiles[0].shape, 0)
    dist = SUBLANES // 2
    while dist >= 1:
        keep_low = (sub & dist) == 0
        nxt = []
        for j in range(len(tiles) // 2):
            lo_t, hi_t = tiles[j], tiles[j + len(tiles) // 2]
            nxt.append(jnp.where(keep_low, lo_t + pltpu.roll(lo_t, SUBLANES - dist, 0),
                                 hi_t + pltpu.roll(hi_t, dist, 0)))
        tiles = nxt
        dist //= 2
    return tiles[0]


def _peer_mix_kernel(alpha, e_cur_ref, e_nxt_ref, gates_ref, x_ref, mod_ref, g_ref, b_ref, uv_ref,
                     o_ref, buf0, buf1, buf2, buf3, h_ref, ffn_ref, sem_ref):
    i = pl.program_id(0)
    n = pl.num_programs(0)
    d = x_ref.shape[1]
    bufs = (buf0, buf1, buf2, buf3)
    ptok = PEER_PHASE_TOKENS

    def start_row(e_ref, row, buf, sem, t, k):
        pltpu.make_async_copy(uv_ref.at[e_ref[row, k]], buf.at[t, k], sem).start(priority=k % 2)

    def wait_buf(p):
        pltpu.make_async_copy(bufs[p], bufs[p], sem_ref.at[p]).wait()

    @pl.when(i == 0)
    def _():
        for p in range(2):
            for t in range(ptok):
                for k in range(PEER_PICKS):
                    start_row(e_cur_ref, p * ptok + t, bufs[p], sem_ref.at[p], t, k)

    x = x_ref[...]
    h = x * (1.0 + mod_ref[4:5, :]) + mod_ref[3:4, :]
    for s in range(d // LANES):
        h_ref[:, s, :] = h[:, s * LANES:(s + 1) * LANES]
    lane = lax.broadcasted_iota(jnp.int32, (SUBLANES, PEER_PICKS), 1)
    sub = lax.broadcasted_iota(jnp.int32, (SUBLANES, PEER_PICKS), 0)

    for p in range(PEER_PHASES):
        q = (p + 2) % PEER_PHASES
        ahead_ref, ahead_row0 = (e_cur_ref, (p + 2) * ptok) if p + 2 < PEER_PHASES else (e_nxt_ref, q * ptok)
        wait_buf(p)
        for t in range(ptok):
            r = p * ptok + t
            h_lo = h_ref[r, 0:SUBLANES, :]
            h_hi = h_ref[r, SUBLANES:, :]
            grow = gates_ref[r:r + 1, :]
            y_lo = jnp.zeros((SUBLANES, LANES), F32)
            y_hi = jnp.zeros((SUBLANES, LANES), F32)
            for g in range(PEER_PICKS // SUBLANES):
                k0 = g * SUBLANES
                prods = []
                for j in range(SUBLANES):
                    u_lo, u_hi = _unpack_pair(bufs[p][t, k0 + j, 0:SUBLANES, :])
                    start_row(ahead_ref, ahead_row0 + t, bufs[q], sem_ref.at[q], t, k0 + j)
                    prods.append(u_lo * h_lo + u_hi * h_hi)
                a = jnp.sum(_sublane_sums(prods), axis=1, keepdims=True)
                gate = jnp.sum(jnp.where(lane == sub + k0, grow, 0.0), axis=1, keepdims=True)
                w = jnp.broadcast_to(gate * _gelu_tanh(a), (SUBLANES, LANES))
                for j in range(SUBLANES):
                    v_lo, v_hi = _unpack_pair(bufs[p][t, k0 + j, SUBLANES:, :])
                    wj = jnp.broadcast_to(w[j:j + 1, :], (SUBLANES, LANES))
                    y_lo = y_lo + wj * v_lo
                    y_hi = y_hi + wj * v_hi
            ffn_ref[r, 0:SUBLANES, :] = y_lo
            ffn_ref[r, SUBLANES:, :] = y_hi

    @pl.when(i == n - 1)
    def _():
        wait_buf(0)
        wait_buf(1)

    ffn = jnp.concatenate([ffn_ref[:, s, :] for s in range(d // LANES)], axis=1)
    g2 = mod_ref[5:6, :]
    o_ref[...] = _layer_norm(alpha * x + (1.0 + g2) * ffn, g_ref[...], b_ref[...])


def _peer_mix(experts, gates, x2, mod_l, ln_g, ln_b, uvp, seq, alpha):
    t, d = x2.shape
    tb = PEER_TOKENS
    nblk = t // tb
    per_b = seq // tb
    const = lambda i: (0, 0)
    return pl.pallas_call(
        functools.partial(_peer_mix_kernel, alpha),
        out_shape=jax.ShapeDtypeStruct((t, d), F32),
        grid=(nblk,),
        in_specs=[
            pl.BlockSpec((tb, PEER_PICKS), lambda i: (i, 0), memory_space=pltpu.SMEM),
            pl.BlockSpec((tb, PEER_PICKS), lambda i: (jnp.minimum(i + 1, nblk - 1), 0),
                         memory_space=pltpu.SMEM),
            pl.BlockSpec((tb, PEER_PICKS), lambda i: (i, 0)),
            pl.BlockSpec((tb, d), lambda i: (i, 0)),
            pl.BlockSpec((None, 6, d), lambda i: (i // per_b, 0, 0)),
            pl.BlockSpec((1, d), const),
            pl.BlockSpec((1, d), const),
            pl.BlockSpec(memory_space=pl.ANY),
        ],
        out_specs=pl.BlockSpec((tb, d), lambda i: (i, 0)),
        scratch_shapes=[pltpu.VMEM((PEER_PHASE_TOKENS, PEER_PICKS, d // LANES, LANES), jnp.uint32)] * PEER_PHASES + [
                        pltpu.VMEM((tb, d // LANES, LANES), F32),
                        pltpu.VMEM((tb, d // LANES, LANES), F32),
                        pltpu.SemaphoreType.DMA((PEER_PHASES,))],
        compiler_params=_cparams("arbitrary"),
        name="peer_mix",
    )(experts, experts, gates, x2, mod_l, ln_g, ln_b, uvp)


PACK_ROWS = 512


def _bf16_bits_high(a):
    b = lax.bitcast_convert_type(a, jnp.uint32)
    return (b + jnp.uint32(0x7FFF) + ((b >> 16) & jnp.uint32(1))) & jnp.uint32(0xFFFF0000)


def _pack_table_kernel(u_ref, v_ref, o_ref):
    hd = u_ref.shape[1] // 2
    tiles = hd // LANES
    for part, src in enumerate((u_ref, v_ref)):
        words = (_bf16_bits_high(src[:, 0:hd]) >> 16) | _bf16_bits_high(src[:, hd:])
        for s in range(tiles):
            o_ref[:, part * tiles + s, :] = words[:, s * LANES:(s + 1) * LANES]


def _pack_table(peer_u, peer_v):
    depth, n, d = peer_u.shape
    spec = pl.BlockSpec((None, PACK_ROWS, d), lambda l, i: (l, i, 0))
    return pl.pallas_call(
        _pack_table_kernel,
        out_shape=jax.ShapeDtypeStruct((depth, n, d // LANES, LANES), jnp.uint32),
        grid=(depth, n // PACK_ROWS),
        in_specs=[spec, spec],
        out_specs=pl.BlockSpec((None, PACK_ROWS, d // LANES, LANES), lambda l, i: (l, i, 0, 0)),
        compiler_params=_cparams("arbitrary", "arbitrary"),
        name="pack_table",
    )(peer_u, peer_v)


def kernel(x, c, ada_w, ada_b, w_in, w_gate2, b_gate, gla_norm_g, conv_w, conv_norm_g, w_out, ln1_g, ln1_b,
           peer_wq, peer_keys, peer_u, peer_v, ln2_g, ln2_b):
    bsz, seq, d = x.shape
    depth = ada_w.shape[0]
    alpha = (2.0 * depth) ** 0.25
    t = bsz * seq

    mod = _ada_mod(c, ada_w, ada_b).reshape(depth, bsz, 6, d)

    qkvr = 2 * GLA_KEY_WIDTH + 2 * GLA_WIDTH
    w_alr = jnp.pad(w_in[:, :, qkvr:qkvr + GLA_GATE_RANK], ((0, 0), (0, 0), (0, ALR_PAD - GLA_GATE_RANK)))
    w_in_p = jnp.concatenate([w_in[:, :, :qkvr], w_in[:, :, qkvr + GLA_GATE_RANK:], w_alr], axis=-1).astype(BF16)
    wg_p = jnp.pad(w_gate2, ((0, 0), (0, ALR_PAD - GLA_GATE_RANK), (0, 0)))
    w_out_b = w_out.astype(BF16)
    wq_b = peer_wq.astype(BF16)
    keys_g = peer_keys.reshape(depth, 2 * PEER_HEADS, N_KEYS, PEER_HALF).astype(BF16)
    uv = _pack_table(peer_u, peer_v)

    x2 = x.reshape(t, d)
    for l in range(depth):
        proj = _in_proj(x2, mod[l], w_in_p[l], seq)
        y = _gla_conv(proj, wg_p[l], b_gate[l][None], gla_norm_g[l][None], conv_w[l], conv_norm_g[l][None],
                      bsz, seq)
        x2 = _out_proj_ln(y, x2, mod[l], w_out_b[l], ln1_g[l][None], ln1_b[l][None], seq, alpha)
        s = _peer_scores(x2, mod[l], wq_b[l], keys_g[l], seq)
        experts, gates = _peer_topk(s)
        x2 = _peer_mix(experts, gates, x2, mod[l], ln2_g[l][None], ln2_b[l][None], uv[l], seq, alpha)
    return x2.reshape(bsz, seq, d)
```

```python
import functools
import math

import jax
import jax.numpy as jnp
from jax import lax
from jax.experimental import pallas as pl
from jax.experimental.pallas import tpu as pltpu

CHUNK = 64
GLA_HEADS = 4
GLA_DK = 128
GLA_DV = 256
GLA_KEY_WIDTH = GLA_HEADS * GLA_DK
GLA_WIDTH = GLA_HEADS * GLA_DV
GLA_GATE_RANK = 16
GLA_GATE_TAU = 16.0
CONV_WIDTH = 1024
PEER_HEADS = 8
N_KEYS = 128
PEER_TOPK = 16
PEER_HALF = 128
EPS = 1e-5

LANES = 128
SUBLANES = 8
VMEM_LIMIT_BYTES = 56 * 1024 * 1024

ALR_PAD = LANES
PROJ_WIDTH = 2 * GLA_KEY_WIDTH + 2 * GLA_WIDTH + 3 * CONV_WIDTH + ALR_PAD

F32 = jnp.float32
BF16 = jnp.bfloat16
NEG_INF = float("-inf")


def _cparams(*sem):
    return pltpu.CompilerParams(dimension_semantics=sem, vmem_limit_bytes=VMEM_LIMIT_BYTES)


def _layer_norm(z, g, b):
    mu = jnp.mean(z, axis=-1, keepdims=True)
    zc = z - mu
    var = jnp.mean(zc * zc, axis=-1, keepdims=True)
    return zc * lax.rsqrt(var + EPS) * g + b


def _rms_norm(z, g):
    return z * lax.rsqrt(jnp.mean(z * z, axis=-1, keepdims=True) + EPS) * g


def _sigmoid(z):
    return 1.0 / (1.0 + jnp.exp(-z))


ADA_COLS = 1024


def _ada_kernel(c_ref, w_ref, b_ref, o_ref):
    c = c_ref[...]
    c_act = (c * _sigmoid(c)).astype(BF16)
    o_ref[...] = jnp.dot(c_act, w_ref[...].astype(BF16), preferred_element_type=F32) + b_ref[...]


def _ada_mod(c, ada_w, ada_b):
    depth, d, n = ada_w.shape
    bsz = c.shape[0]
    tn = ADA_COLS
    return pl.pallas_call(
        _ada_kernel,
        out_shape=jax.ShapeDtypeStruct((depth, bsz, n), F32),
        grid=(depth, n // tn),
        in_specs=[
            pl.BlockSpec((bsz, d), lambda l, j: (0, 0)),
            pl.BlockSpec((None, d, tn), lambda l, j: (l, 0, j)),
            pl.BlockSpec((None, 1, tn), lambda l, j: (l, 0, j)),
        ],
        out_specs=pl.BlockSpec((None, bsz, tn), lambda l, j: (l, 0, j)),
        compiler_params=_cparams("arbitrary", "arbitrary"),
        name="ada_mod",
    )(c, ada_w, ada_b.reshape(depth, 1, n))


IN_PROJ_ROWS = 1024
IN_PROJ_COLS = 896
ROW_TILE = 512


W_PREP_ROWS = 256


def _w_in_prep_kernel(w_ref, o_ref):
    qkvr = 2 * GLA_KEY_WIDTH + 2 * GLA_WIDTH
    conv = 3 * CONV_WIDTH
    rows = w_ref.shape[0]
    o_ref[:, 0:qkvr] = w_ref[:, 0:qkvr].astype(BF16)
    o_ref[:, qkvr:qkvr + conv] = w_ref[:, qkvr + GLA_GATE_RANK:qkvr + GLA_GATE_RANK + conv].astype(BF16)
    alr = jnp.concatenate([w_ref[:, qkvr:qkvr + GLA_GATE_RANK],
                           jnp.zeros((rows, ALR_PAD - GLA_GATE_RANK), F32)], axis=1)
    o_ref[:, qkvr + conv:] = alr.astype(BF16)


def _w_in_prep(w_in):
    depth, d, n = w_in.shape
    return pl.pallas_call(
        _w_in_prep_kernel,
        out_shape=jax.ShapeDtypeStruct((depth, d, PROJ_WIDTH), BF16),
        grid=(depth, d // W_PREP_ROWS),
        in_specs=[pl.BlockSpec((None, W_PREP_ROWS, n), lambda l, i: (l, i, 0))],
        out_specs=pl.BlockSpec((None, W_PREP_ROWS, PROJ_WIDTH), lambda l, i: (l, i, 0)),
        compiler_params=_cparams("arbitrary", "arbitrary"),
        name="w_in_prep",
    )(w_in)


def _in_proj_kernel(x_ref, mod_ref, w_ref, o_ref, h_ref):
    @pl.when(pl.program_id(1) == 0)
    def _():
        sh = mod_ref[0:1, :]
        sc = mod_ref[1:2, :]
        h_ref[...] = (x_ref[...] * (1.0 + sc) + sh).astype(BF16)

    o_ref[...] = jnp.dot(h_ref[...], w_ref[...], preferred_element_type=F32)


def _in_proj(x2, mod_l, w_in_p, seq):
    t, d = x2.shape
    n = w_in_p.shape[1]
    tm, tn = min(IN_PROJ_ROWS, seq), IN_PROJ_COLS
    per_b = seq // tm
    return pl.pallas_call(
        _in_proj_kernel,
        out_shape=jax.ShapeDtypeStruct((t, n), F32),
        grid=(t // tm, n // tn),
        in_specs=[
            pl.BlockSpec((tm, d), lambda i, j: (i, 0)),
            pl.BlockSpec((None, 6, d), lambda i, j: (i // per_b, 0, 0)),
            pl.BlockSpec((d, tn), lambda i, j: (0, j)),
        ],
        out_specs=pl.BlockSpec((tm, tn), lambda i, j: (i, j)),
        scratch_shapes=[pltpu.VMEM((tm, d), BF16)],
        compiler_params=_cparams("arbitrary", "arbitrary"),
        name="in_proj",
    )(x2, mod_l, w_in_p)


GLA_ROWS = 256


def _gla_conv_kernel(q_ref, k_ref, v_ref, r_ref, cb_ref, cc_ref, ch_ref, alr_ref,
                     wg_ref, bg_ref, gng_ref, cw_ref, cng_ref, y_ref, state_ref, carry_ref):
    @pl.when(pl.program_id(1) == 0)
    def _():
        state_ref[...] = jnp.zeros_like(state_ref)
        carry_ref[...] = jnp.zeros_like(carry_ref)

    logit = jnp.dot(alr_ref[...], wg_ref[...], preferred_element_type=F32,
                    precision=lax.Precision.HIGHEST) + bg_ref[...]
    log_a = (jnp.minimum(logit, 0.0) - jnp.log(1.0 + jnp.exp(-jnp.abs(logit)))) * (1.0 / GLA_GATE_TAU)
    row = lax.broadcasted_iota(jnp.int32, (CHUNK, CHUNK), 0)
    col = lax.broadcasted_iota(jnp.int32, (CHUNK, CHUNK), 1)
    tri = (col <= row).astype(F32)
    gng = gng_ref[...]
    scale = GLA_DK ** -0.5
    for c in range(GLA_ROWS // CHUNK):
        rows = slice(c * CHUNK, (c + 1) * CHUNK)
        cum = jnp.dot(tri, log_a[rows], preferred_element_type=F32, precision=lax.Precision.HIGHEST)
        total = cum[CHUNK - 1:CHUNK, :]
        k_dec = (k_ref[rows, :] * jnp.exp(total - cum)).astype(BF16)
        dec = jnp.exp(total)
        qs = (q_ref[rows, :] * scale).astype(BF16)
        for h in range(GLA_HEADS):
            ks = slice(h * GLA_DK, (h + 1) * GLA_DK)
            vs = slice(h * GLA_DV, (h + 1) * GLA_DV)
            vh = v_ref[rows, vs].astype(BF16)
            upd = lax.dot_general(vh, k_dec[:, ks], (((0,), (0,)), ((), ())), preferred_element_type=F32)
            st = state_ref[h] * dec[:, ks] + upd
            state_ref[h] = st
            o = lax.dot_general(qs[:, ks], st.astype(BF16), (((1,), (1,)), ((), ())),
                                preferred_element_type=F32)
            rh = r_ref[rows, vs]
            y_ref[rows, vs] = (_rms_norm(o, gng) * (rh * _sigmoid(rh))).astype(y_ref.dtype)

    u = cc_ref[...] * ch_ref[...]
    prev = carry_ref[...]
    ridx = lax.broadcasted_iota(jnp.int32, (GLA_ROWS, 1), 0)
    u1 = jnp.where(ridx == 0, prev[SUBLANES - 1:SUBLANES, :], pltpu.roll(u, 1, 0))
    u2 = jnp.where(ridx == 0, prev[SUBLANES - 2:SUBLANES - 1, :],
                   jnp.where(ridx == 1, prev[SUBLANES - 1:SUBLANES, :], pltpu.roll(u, 2, 0)))
    conv = cw_ref[0:1, :] * u2 + cw_ref[1:2, :] * u1 + cw_ref[2:3, :] * u
    carry_ref[...] = u[GLA_ROWS - SUBLANES:, :]
    y_ref[:, GLA_WIDTH:] = _rms_norm(cb_ref[...] * conv, cng_ref[...]).astype(y_ref.dtype)


def _gla_conv(proj, wg_p, bg, gng, cw, cng, bsz, seq):
    t = proj.shape[0]
    nblk = seq // GLA_ROWS
    rmap = lambda width_blocks: (lambda b, j: (b * nblk + j, width_blocks))
    kw, vw = GLA_KEY_WIDTH, GLA_WIDTH
    const = lambda b, j: (0, 0)
    return pl.pallas_call(
        _gla_conv_kernel,
        out_shape=jax.ShapeDtypeStruct((t, GLA_WIDTH + CONV_WIDTH), BF16),
        grid=(bsz, nblk),
        in_specs=[
            pl.BlockSpec((GLA_ROWS, kw), rmap(0)),
            pl.BlockSpec((GLA_ROWS, kw), rmap(1)),
            pl.BlockSpec((GLA_ROWS, vw), rmap(1)),
            pl.BlockSpec((GLA_ROWS, vw), rmap(2)),
            pl.BlockSpec((GLA_ROWS, CONV_WIDTH), rmap(3)),
            pl.BlockSpec((GLA_ROWS, CONV_WIDTH), rmap(4)),
            pl.BlockSpec((GLA_ROWS, CONV_WIDTH), rmap(5)),
            pl.BlockSpec((GLA_ROWS, ALR_PAD), rmap((PROJ_WIDTH - ALR_PAD) // ALR_PAD)),
            pl.BlockSpec((ALR_PAD, kw), const),
            pl.BlockSpec((1, kw), const),
            pl.BlockSpec((1, GLA_DV), const),
            pl.BlockSpec((3, CONV_WIDTH), const),
            pl.BlockSpec((1, CONV_WIDTH), const),
        ],
        out_specs=pl.BlockSpec((GLA_ROWS, GLA_WIDTH + CONV_WIDTH), lambda b, j: (b * nblk + j, 0)),
        scratch_shapes=[pltpu.VMEM((GLA_HEADS, GLA_DV, GLA_DK), F32),
                        pltpu.VMEM((SUBLANES, CONV_WIDTH), F32)],
        compiler_params=_cparams("arbitrary", "arbitrary"),
        name="gla_conv",
    )(proj, proj, proj, proj, proj, proj, proj, proj, wg_p, bg, gng, cw, cng)


def _out_proj_ln_kernel(alpha, y_ref, x_ref, mod_ref, w_ref, g_ref, b_ref, o_ref):
    mix = jnp.dot(y_ref[...], w_ref[...], preferred_element_type=F32)
    g1 = mod_ref[2:3, :]
    o_ref[...] = _layer_norm(alpha * x_ref[...] + (1.0 + g1) * mix, g_ref[...], b_ref[...])


def _out_proj_ln(y, x2, mod_l, w_out, ln_g, ln_b, seq, alpha):
    t, d = x2.shape
    tm = ROW_TILE
    per_b = seq // tm
    const = lambda i: (0, 0)
    return pl.pallas_call(
        functools.partial(_out_proj_ln_kernel, alpha),
        out_shape=jax.ShapeDtypeStruct((t, d), F32),
        grid=(t // tm,),
        in_specs=[
            pl.BlockSpec((tm, y.shape[1]), lambda i: (i, 0)),
            pl.BlockSpec((tm, d), lambda i: (i, 0)),
            pl.BlockSpec((None, 6, d), lambda i: (i // per_b, 0, 0)),
            pl.BlockSpec(w_out.shape, const),
            pl.BlockSpec((1, d), const),
            pl.BlockSpec((1, d), const),
        ],
        out_specs=pl.BlockSpec((tm, d), lambda i: (i, 0)),
        compiler_params=_cparams("arbitrary"),
        name="out_proj_ln",
    )(y, x2, mod_l, w_out, ln_g, ln_b)


def _peer_scores_kernel(x_ref, mod_ref, wq_ref, keys_ref, s_ref):
    sh = mod_ref[3:4, :]
    sc = mod_ref[4:5, :]
    h = (x_ref[...] * (1.0 + sc) + sh).astype(BF16)
    q = jnp.dot(h, wq_ref[...], preferred_element_type=F32).astype(BF16)
    for g in range(2 * PEER_HEADS):
        qg = q[:, g * PEER_HALF:(g + 1) * PEER_HALF]
        s_ref[g] = lax.dot_general(keys_ref[g], qg, (((1,), (1,)), ((), ())), preferred_element_type=F32)


def _peer_scores(x2, mod_l, wq, keys_g, seq):
    t, d = x2.shape
    tm = ROW_TILE
    per_b = seq // tm
    ng = 2 * PEER_HEADS
    return pl.pallas_call(
        _peer_scores_kernel,
        out_shape=jax.ShapeDtypeStruct((ng, N_KEYS, t), F32),
        grid=(t // tm,),
        in_specs=[
            pl.BlockSpec((tm, d), lambda i: (i, 0)),
            pl.BlockSpec((None, 6, d), lambda i: (i // per_b, 0, 0)),
            pl.BlockSpec(wq.shape, lambda i: (0, 0)),
            pl.BlockSpec(keys_g.shape, lambda i: (0, 0, 0)),
        ],
        out_specs=pl.BlockSpec((ng, N_KEYS, tm), lambda i: (0, 0, i)),
        compiler_params=_cparams("arbitrary"),
        name="peer_scores",
    )(x2, mod_l, wq, keys_g)


TOPK_TOKENS = LANES


def _top16_rows(s, ids, big):
    vals, out_ids = [], []
    for _ in range(PEER_TOPK):
        m = jnp.max(s, axis=0, keepdims=True)
        idx = jnp.min(jnp.where(s == m, ids, big), axis=0, keepdims=True)
        vals.append(m)
        out_ids.append(idx)
        s = jnp.where(ids == idx, NEG_INF, s)
    return jnp.concatenate(vals, axis=0), jnp.concatenate(out_ids, axis=0)


def _select_rows(table, sel):
    out = jnp.zeros(sel.shape, table.dtype)
    for a in range(PEER_TOPK):
        out = jnp.where(sel == a, table[a:a + 1, :], out)
    return out


def _pair_candidates(v1, v2):
    sub = lax.broadcasted_iota(jnp.int32, (SUBLANES,) + v1.shape[1:], 0)
    cand, flat = [], []
    for b0 in (0, SUBLANES):
        cand.append(v1[0:1, :] + v2[b0:b0 + SUBLANES, :])
        flat.append(sub + b0)
    for a in range(1, SUBLANES):
        cand.append(v1[a:a + 1, :] + v2[0:SUBLANES, :])
        flat.append(sub + a * PEER_TOPK)
    cand.append(v1[SUBLANES:, :] + v2[0:1, :])
    flat.append((sub + SUBLANES) * PEER_TOPK)
    return jnp.concatenate(cand, axis=0), jnp.concatenate(flat, axis=0)


def _peer_topk_kernel(s_ref, e_ref, g_ref, et_ref, gt_ref):
    key_ids = lax.broadcasted_iota(jnp.int32, s_ref.shape[1:], 0)

    def head(hd):
        v1, i1 = _top16_rows(s_ref[2 * hd], key_ids, N_KEYS)
        v2, i2 = _top16_rows(s_ref[2 * hd + 1], key_ids, N_KEYS)
        cand, flat = _pair_candidates(v1, v2)
        cv, ci = _top16_rows(cand, flat, PEER_TOPK * PEER_TOPK)
        e1 = _select_rows(i1, ci // PEER_TOPK)
        e2 = _select_rows(i2, ci % PEER_TOPK)
        p = jnp.exp(cv - cv[0:1, :])
        gates = p / jnp.sum(p, axis=0, keepdims=True)
        rows = pl.ds(pl.multiple_of(hd * PEER_TOPK, PEER_TOPK), PEER_TOPK)
        et_ref[rows, :] = e1 * N_KEYS + e2
        gt_ref[rows, :] = gates

    for hd in range(PEER_HEADS):
        head(hd)
    e_ref[...] = et_ref[...].T
    g_ref[...] = gt_ref[...].T


def _peer_topk(s):
    ng, nk, t = s.shape
    tt = TOPK_TOKENS
    width = PEER_HEADS * PEER_TOPK
    return pl.pallas_call(
        _peer_topk_kernel,
        out_shape=(jax.ShapeDtypeStruct((t, width), jnp.int32), jax.ShapeDtypeStruct((t, width), F32)),
        grid=(t // tt,),
        in_specs=[pl.BlockSpec((ng, nk, tt), lambda i: (0, 0, i))],
        out_specs=(pl.BlockSpec((tt, width), lambda i: (i, 0)), pl.BlockSpec((tt, width), lambda i: (i, 0))),
        scratch_shapes=[pltpu.VMEM((width, tt), jnp.int32), pltpu.VMEM((width, tt), F32)],
        compiler_params=_cparams("arbitrary"),
        name="peer_topk",
    )(s)


PEER_PHASES = 4
PEER_PHASE_TOKENS = 4
PEER_TOKENS = PEER_PHASES * PEER_PHASE_TOKENS
PEER_PICKS = PEER_HEADS * PEER_TOPK


def _gelu_tanh(z):
    return 0.5 * z * (1.0 + jnp.tanh(math.sqrt(2.0 / math.pi) * (z + 0.044715 * (z * z * z))))


def _unpack_pair(w32):
    lo = lax.bitcast_convert_type(w32 << 16, F32)
    hi = lax.bitcast_convert_type(w32 & jnp.uint32(0xFFFF0000), F32)
    return lo, hi


def _sublane_sums(tiles):
    assert len(tiles) == SUBLANES
    sub = lax.broadcasted_iota(jnp.int32, tiles[0].shape, 0)
    dist = SUBLANES // 2
    while dist >= 1:
        keep_low = (sub & dist) == 0
        nxt = []
        for j in range(len(tiles) // 2):
            lo_t, hi_t = tiles[j], tiles[j + len(tiles) // 2]
            nxt.append(jnp.where(keep_low, lo_t + pltpu.roll(lo_t, SUBLANES - dist, 0),
                                 hi_t + pltpu.roll(hi_t, dist, 0)))
        tiles = nxt
        dist //= 2
    return tiles[0]


def _peer_mix_kernel(alpha, e_cur_ref, e_nxt_ref, gates_ref, x_ref, mod_ref, g_ref, b_ref, uv_ref,
                     o_ref, buf0, buf1, buf2, buf3, h_ref, ffn_ref, sem_ref):
    i = pl.program_id(0)
    n = pl.num_programs(0)
    d = x_ref.shape[1]
    bufs = (buf0, buf1, buf2, buf3)
    ptok = PEER_PHASE_TOKENS

    def start_row(e_ref, row, buf, sem, t, k):
        pltpu.make_async_copy(uv_ref.at[e_ref[row, k]], buf.at[t, k], sem).start(priority=k % 2)

    def wait_buf(p):
        pltpu.make_async_copy(bufs[p], bufs[p], sem_ref.at[p]).wait()

    @pl.when(i == 0)
    def _():
        for p in range(2):
            for t in range(ptok):
                for k in range(PEER_PICKS):
                    start_row(e_cur_ref, p * ptok + t, bufs[p], sem_ref.at[p], t, k)

    x = x_ref[...]
    h = x * (1.0 + mod_ref[4:5, :]) + mod_ref[3:4, :]
    for s in range(d // LANES):
        h_ref[:, s, :] = h[:, s * LANES:(s + 1) * LANES]
    lane = lax.broadcasted_iota(jnp.int32, (SUBLANES, PEER_PICKS), 1)
    sub = lax.broadcasted_iota(jnp.int32, (SUBLANES, PEER_PICKS), 0)

    for p in range(PEER_PHASES):
        q = (p + 2) % PEER_PHASES
        ahead_ref, ahead_row0 = (e_cur_ref, (p + 2) * ptok) if p + 2 < PEER_PHASES else (e_nxt_ref, q * ptok)
        wait_buf(p)
        for t in range(ptok):
            r = p * ptok + t
            h_lo = h_ref[r, 0:SUBLANES, :]
            h_hi = h_ref[r, SUBLANES:, :]
            grow = gates_ref[r:r + 1, :]
            y_lo = jnp.zeros((SUBLANES, LANES), F32)
            y_hi = jnp.zeros((SUBLANES, LANES), F32)
            for g in range(PEER_PICKS // SUBLANES):
                k0 = g * SUBLANES
                prods = []
                for j in range(SUBLANES):
                    u_lo, u_hi = _unpack_pair(bufs[p][t, k0 + j, 0:SUBLANES, :])
                    start_row(ahead_ref, ahead_row0 + t, bufs[q], sem_ref.at[q], t, k0 + j)
                    prods.append(u_lo * h_lo + u_hi * h_hi)
                a = jnp.sum(_sublane_sums(prods), axis=1, keepdims=True)
                gate = jnp.sum(jnp.where(lane == sub + k0, grow, 0.0), axis=1, keepdims=True)
                w = jnp.broadcast_to(gate * _gelu_tanh(a), (SUBLANES, LANES))
                for j in range(SUBLANES):
                    v_lo, v_hi = _unpack_pair(bufs[p][t, k0 + j, SUBLANES:, :])
                    wj = jnp.broadcast_to(w[j:j + 1, :], (SUBLANES, LANES))
                    y_lo = y_lo + wj * v_lo
                    y_hi = y_hi + wj * v_hi
            ffn_ref[r, 0:SUBLANES, :] = y_lo
            ffn_ref[r, SUBLANES:, :] = y_hi

    @pl.when(i == n - 1)
    def _():
        wait_buf(0)
        wait_buf(1)

    ffn = jnp.concatenate([ffn_ref[:, s, :] for s in range(d // LANES)], axis=1)
    g2 = mod_ref[5:6, :]
    o_ref[...] = _layer_norm(alpha * x + (1.0 + g2) * ffn, g_ref[...], b_ref[...])


def _peer_mix(experts, gates, x2, mod_l, ln_g, ln_b, uvp, seq, alpha):
    t, d = x2.shape
    tb = PEER_TOKENS
    nblk = t // tb
    per_b = seq // tb
    const = lambda i: (0, 0)
    return pl.pallas_call(
        functools.partial(_peer_mix_kernel, alpha),
        out_shape=jax.ShapeDtypeStruct((t, d), F32),
        grid=(nblk,),
        in_specs=[
            pl.BlockSpec((tb, PEER_PICKS), lambda i: (i, 0), memory_space=pltpu.SMEM),
            pl.BlockSpec((tb, PEER_PICKS), lambda i: (jnp.minimum(i + 1, nblk - 1), 0),
                         memory_space=pltpu.SMEM),
            pl.BlockSpec((tb, PEER_PICKS), lambda i: (i, 0)),
            pl.BlockSpec((tb, d), lambda i: (i, 0)),
            pl.BlockSpec((None, 6, d), lambda i: (i // per_b, 0, 0)),
            pl.BlockSpec((1, d), const),
            pl.BlockSpec((1, d), const),
            pl.BlockSpec(memory_space=pl.ANY),
        ],
        out_specs=pl.BlockSpec((tb, d), lambda i: (i, 0)),
        scratch_shapes=[pltpu.VMEM((PEER_PHASE_TOKENS, PEER_PICKS, d // LANES, LANES), jnp.uint32)] * PEER_PHASES + [
                        pltpu.VMEM((tb, d // LANES, LANES), F32),
                        pltpu.VMEM((tb, d // LANES, LANES), F32),
                        pltpu.SemaphoreType.DMA((PEER_PHASES,))],
        compiler_params=_cparams("arbitrary"),
        name="peer_mix",
    )(experts, experts, gates, x2, mod_l, ln_g, ln_b, uvp)


PACK_ROWS = 512


def _bf16_bits_high(a):
    b = lax.bitcast_convert_type(a, jnp.uint32)
    return (b + jnp.uint32(0x7FFF) + ((b >> 16) & jnp.uint32(1))) & jnp.uint32(0xFFFF0000)


def _pack_table_kernel(u_ref, v_ref, o_ref):
    hd = u_ref.shape[1] // 2
    tiles = hd // LANES
    for part, src in enumerate((u_ref, v_ref)):
        words = (_bf16_bits_high(src[:, 0:hd]) >> 16) | _bf16_bits_high(src[:, hd:])
        for s in range(tiles):
            o_ref[:, part * tiles + s, :] = words[:, s * LANES:(s + 1) * LANES]


def _pack_table(peer_u, peer_v):
    depth, n, d = peer_u.shape
    spec = pl.BlockSpec((None, PACK_ROWS, d), lambda l, i: (l, i, 0))
    return pl.pallas_call(
        _pack_table_kernel,
        out_shape=jax.ShapeDtypeStruct((depth, n, d // LANES, LANES), jnp.uint32),
        grid=(depth, n // PACK_ROWS),
        in_specs=[spec, spec],
        out_specs=pl.BlockSpec((None, PACK_ROWS, d // LANES, LANES), lambda l, i: (l, i, 0, 0)),
        compiler_params=_cparams("arbitrary", "arbitrary"),
        name="pack_table",
    )(peer_u, peer_v)


def kernel(x, c, ada_w, ada_b, w_in, w_gate2, b_gate, gla_norm_g, conv_w, conv_norm_g, w_out, ln1_g, ln1_b,
           peer_wq, peer_keys, peer_u, peer_v, ln2_g, ln2_b):
    bsz, seq, d = x.shape
    depth = ada_w.shape[0]
    alpha = (2.0 * depth) ** 0.25
    t = bsz * seq

    mod = _ada_mod(c, ada_w, ada_b).reshape(depth, bsz, 6, d)

    w_in_p = _w_in_prep(w_in)
    wg_p = jnp.pad(w_gate2, ((0, 0), (0, ALR_PAD - GLA_GATE_RANK), (0, 0)))
    w_out_b = w_out.astype(BF16)
    wq_b = peer_wq.astype(BF16)
    keys_g = peer_keys.reshape(depth, 2 * PEER_HEADS, N_KEYS, PEER_HALF).astype(BF16)
    uv = _pack_table(peer_u, peer_v)

    x2 = x.reshape(t, d)
    for l in range(depth):
        proj = _in_proj(x2, mod[l], w_in_p[l], seq)
        y = _gla_conv(proj, wg_p[l], b_gate[l][None], gla_norm_g[l][None], conv_w[l], conv_norm_g[l][None],
                      bsz, seq)
        x2 = _out_proj_ln(y, x2, mod[l], w_out_b[l], ln1_g[l][None], ln1_b[l][None], seq, alpha)
        s = _peer_scores(x2, mod[l], wq_b[l], keys_g[l], seq)
        experts, gates = _peer_topk(s)
        x2 = _peer_mix(experts, gates, x2, mod[l], ln2_g[l][None], ln2_b[l][None], uv[l], seq, alpha)
    return x2.reshape(bsz, seq, d)
```

```python
import functools
import math

import jax
import jax.numpy as jnp
from jax import lax
from jax.experimental import pallas as pl
from jax.experimental.pallas import tpu as pltpu

CHUNK = 64
GLA_HEADS = 4
GLA_DK = 128
GLA_DV = 256
GLA_KEY_WIDTH = GLA_HEADS * GLA_DK
GLA_WIDTH = GLA_HEADS * GLA_DV
GLA_GATE_RANK = 16
GLA_GATE_TAU = 16.0
CONV_WIDTH = 1024
PEER_HEADS = 8
N_KEYS = 128
PEER_TOPK = 16
PEER_HALF = 128
EPS = 1e-5

LANES = 128
SUBLANES = 8
VMEM_LIMIT_BYTES = 56 * 1024 * 1024

ALR_PAD = LANES
PROJ_WIDTH = 2 * GLA_KEY_WIDTH + 2 * GLA_WIDTH + 3 * CONV_WIDTH + ALR_PAD

F32 = jnp.float32
BF16 = jnp.bfloat16
NEG_INF = float("-inf")


def _cparams(*sem):
    return pltpu.CompilerParams(dimension_semantics=sem, vmem_limit_bytes=VMEM_LIMIT_BYTES)


def _layer_norm(z, g, b):
    mu = jnp.mean(z, axis=-1, keepdims=True)
    zc = z - mu
    var = jnp.mean(zc * zc, axis=-1, keepdims=True)
    return zc * lax.rsqrt(var + EPS) * g + b


def _rms_norm(z, g):
    return z * lax.rsqrt(jnp.mean(z * z, axis=-1, keepdims=True) + EPS) * g


def _sigmoid(z):
    return 1.0 / (1.0 + jnp.exp(-z))


ADA_COLS = 1024


def _ada_kernel(c_ref, w_ref, b_ref, o_ref):
    c = c_ref[...]
    c_act = (c * _sigmoid(c)).astype(BF16)
    o_ref[...] = jnp.dot(c_act, w_ref[...].astype(BF16), preferred_element_type=F32) + b_ref[...]


def _ada_mod(c, ada_w, ada_b):
    depth, d, n = ada_w.shape
    bsz = c.shape[0]
    tn = ADA_COLS
    return pl.pallas_call(
        _ada_kernel,
        out_shape=jax.ShapeDtypeStruct((depth, bsz, n), F32),
        grid=(depth, n // tn),
        in_specs=[
            pl.BlockSpec((bsz, d), lambda l, j: (0, 0)),
            pl.BlockSpec((None, d, tn), lambda l, j: (l, 0, j)),
            pl.BlockSpec((None, 1, tn), lambda l, j: (l, 0, j)),
        ],
        out_specs=pl.BlockSpec((None, bsz, tn), lambda l, j: (l, 0, j)),
        compiler_params=_cparams("arbitrary", "arbitrary"),
        name="ada_mod",
    )(c, ada_w, ada_b.reshape(depth, 1, n))


IN_PROJ_ROWS = 1024
IN_PROJ_COLS = 896
ROW_TILE = 512


W_PREP_ROWS = 256


def _w_in_prep_kernel(w_ref, o_ref):
    qkvr = 2 * GLA_KEY_WIDTH + 2 * GLA_WIDTH
    conv = 3 * CONV_WIDTH
    rows = w_ref.shape[0]
    o_ref[:, 0:qkvr] = w_ref[:, 0:qkvr].astype(BF16)
    o_ref[:, qkvr:qkvr + conv] = w_ref[:, qkvr + GLA_GATE_RANK:qkvr + GLA_GATE_RANK + conv].astype(BF16)
    alr = jnp.concatenate([w_ref[:, qkvr:qkvr + GLA_GATE_RANK],
                           jnp.zeros((rows, ALR_PAD - GLA_GATE_RANK), F32)], axis=1)
    o_ref[:, qkvr + conv:] = alr.astype(BF16)


def _w_in_prep(w_in):
    depth, d, n = w_in.shape
    return pl.pallas_call(
        _w_in_prep_kernel,
        out_shape=jax.ShapeDtypeStruct((depth, d, PROJ_WIDTH), BF16),
        grid=(depth, d // W_PREP_ROWS),
        in_specs=[pl.BlockSpec((None, W_PREP_ROWS, n), lambda l, i: (l, i, 0))],
        out_specs=pl.BlockSpec((None, W_PREP_ROWS, PROJ_WIDTH), lambda l, i: (l, i, 0)),
        compiler_params=_cparams("arbitrary", "arbitrary"),
        name="w_in_prep",
    )(w_in)


def _in_proj_kernel(x_ref, mod_ref, w_ref, o_ref, h_ref):
    @pl.when(pl.program_id(1) == 0)
    def _():
        sh = mod_ref[0:1, :]
        sc = mod_ref[1:2, :]
        h_ref[...] = (x_ref[...] * (1.0 + sc) + sh).astype(BF16)

    o_ref[...] = jnp.dot(h_ref[...], w_ref[...], preferred_element_type=F32)


def _in_proj(x2, mod_l, w_in_p, seq):
    t, d = x2.shape
    n = w_in_p.shape[1]
    tm, tn = min(IN_PROJ_ROWS, seq), IN_PROJ_COLS
    per_b = seq // tm
    return pl.pallas_call(
        _in_proj_kernel,
        out_shape=jax.ShapeDtypeStruct((t, n), F32),
        grid=(t // tm, n // tn),
        in_specs=[
            pl.BlockSpec((tm, d), lambda i, j: (i, 0)),
            pl.BlockSpec((None, 6, d), lambda i, j: (i // per_b, 0, 0)),
            pl.BlockSpec((d, tn), lambda i, j: (0, j)),
        ],
        out_specs=pl.BlockSpec((tm, tn), lambda i, j: (i, j)),
        scratch_shapes=[pltpu.VMEM((tm, d), BF16)],
        compiler_params=_cparams("arbitrary", "arbitrary"),
        name="in_proj",
    )(x2, mod_l, w_in_p)


GLA_ROWS = 256


def _gla_conv_kernel(q_ref, k_ref, v_ref, r_ref, cb_ref, cc_ref, ch_ref, alr_ref,
                     wg_ref, bg_ref, gng_ref, cw_ref, cng_ref, y_ref, state_ref, carry_ref):
    @pl.when(pl.program_id(1) == 0)
    def _():
        state_ref[...] = jnp.zeros_like(state_ref)
        carry_ref[...] = jnp.zeros_like(carry_ref)

    logit = jnp.dot(alr_ref[...], wg_ref[...], preferred_element_type=F32,
                    precision=lax.Precision.HIGHEST) + bg_ref[...]
    log_a = (jnp.minimum(logit, 0.0) - jnp.log(1.0 + jnp.exp(-jnp.abs(logit)))) * (1.0 / GLA_GATE_TAU)
    row = lax.broadcasted_iota(jnp.int32, (CHUNK, CHUNK), 0)
    col = lax.broadcasted_iota(jnp.int32, (CHUNK, CHUNK), 1)
    tri = (col <= row).astype(F32)
    gng = gng_ref[...]
    scale = GLA_DK ** -0.5
    for c in range(GLA_ROWS // CHUNK):
        rows = slice(c * CHUNK, (c + 1) * CHUNK)
        cum = jnp.dot(tri, log_a[rows], preferred_element_type=F32, precision=lax.Precision.HIGHEST)
        total = cum[CHUNK - 1:CHUNK, :]
        k_dec = (k_ref[rows, :] * jnp.exp(total - cum)).astype(BF16)
        dec = jnp.exp(total)
        qs = (q_ref[rows, :] * scale).astype(BF16)
        for h in range(GLA_HEADS):
            ks = slice(h * GLA_DK, (h + 1) * GLA_DK)
            vs = slice(h * GLA_DV, (h + 1) * GLA_DV)
            vh = v_ref[rows, vs].astype(BF16)
            upd = lax.dot_general(vh, k_dec[:, ks], (((0,), (0,)), ((), ())), preferred_element_type=F32)
            st = state_ref[h] * dec[:, ks] + upd
            state_ref[h] = st
            o = lax.dot_general(qs[:, ks], st.astype(BF16), (((1,), (1,)), ((), ())),
                                preferred_element_type=F32)
            rh = r_ref[rows, vs]
            y_ref[rows, vs] = (_rms_norm(o, gng) * (rh * _sigmoid(rh))).astype(y_ref.dtype)

    u = cc_ref[...] * ch_ref[...]
    prev = carry_ref[...]
    ridx = lax.broadcasted_iota(jnp.int32, (GLA_ROWS, 1), 0)
    u1 = jnp.where(ridx == 0, prev[SUBLANES - 1:SUBLANES, :], pltpu.roll(u, 1, 0))
    u2 = jnp.where(ridx == 0, prev[SUBLANES - 2:SUBLANES - 1, :],
                   jnp.where(ridx == 1, prev[SUBLANES - 1:SUBLANES, :], pltpu.roll(u, 2, 0)))
    conv = cw_ref[0:1, :] * u2 + cw_ref[1:2, :] * u1 + cw_ref[2:3, :] * u
    carry_ref[...] = u[GLA_ROWS - SUBLANES:, :]
    y_ref[:, GLA_WIDTH:] = _rms_norm(cb_ref[...] * conv, cng_ref[...]).astype(y_ref.dtype)


def _gla_conv(proj, wg_p, bg, gng, cw, cng, bsz, seq):
    t = proj.shape[0]
    nblk = seq // GLA_ROWS
    rmap = lambda width_blocks: (lambda b, j: (b * nblk + j, width_blocks))
    kw, vw = GLA_KEY_WIDTH, GLA_WIDTH
    const = lambda b, j: (0, 0)
    return pl.pallas_call(
        _gla_conv_kernel,
        out_shape=jax.ShapeDtypeStruct((t, GLA_WIDTH + CONV_WIDTH), BF16),
        grid=(bsz, nblk),
        in_specs=[
            pl.BlockSpec((GLA_ROWS, kw), rmap(0)),
            pl.BlockSpec((GLA_ROWS, kw), rmap(1)),
            pl.BlockSpec((GLA_ROWS, vw), rmap(1)),
            pl.BlockSpec((GLA_ROWS, vw), rmap(2)),
            pl.BlockSpec((GLA_ROWS, CONV_WIDTH), rmap(3)),
            pl.BlockSpec((GLA_ROWS, CONV_WIDTH), rmap(4)),
            pl.BlockSpec((GLA_ROWS, CONV_WIDTH), rmap(5)),
            pl.BlockSpec((GLA_ROWS, ALR_PAD), rmap((PROJ_WIDTH - ALR_PAD) // ALR_PAD)),
            pl.BlockSpec((ALR_PAD, kw), const),
            pl.BlockSpec((1, kw), const),
            pl.BlockSpec((1, GLA_DV), const),
            pl.BlockSpec((3, CONV_WIDTH), const),
            pl.BlockSpec((1, CONV_WIDTH), const),
        ],
        out_specs=pl.BlockSpec((GLA_ROWS, GLA_WIDTH + CONV_WIDTH), lambda b, j: (b * nblk + j, 0)),
        scratch_shapes=[pltpu.VMEM((GLA_HEADS, GLA_DV, GLA_DK), F32),
                        pltpu.VMEM((SUBLANES, CONV_WIDTH), F32)],
        compiler_params=_cparams("arbitrary", "arbitrary"),
        name="gla_conv",
    )(proj, proj, proj, proj, proj, proj, proj, proj, wg_p, bg, gng, cw, cng)


def _out_proj_ln_kernel(alpha, y_ref, x_ref, mod_ref, w_ref, g_ref, b_ref, o_ref):
    mix = jnp.dot(y_ref[...], w_ref[...], preferred_element_type=F32)
    g1 = mod_ref[2:3, :]
    o_ref[...] = _layer_norm(alpha * x_ref[...] + (1.0 + g1) * mix, g_ref[...], b_ref[...])


def _out_proj_ln(y, x2, mod_l, w_out, ln_g, ln_b, seq, alpha):
    t, d = x2.shape
    tm = ROW_TILE
    per_b = seq // tm
    const = lambda i: (0, 0)
    return pl.pallas_call(
        functools.partial(_out_proj_ln_kernel, alpha),
        out_shape=jax.ShapeDtypeStruct((t, d), F32),
        grid=(t // tm,),
        in_specs=[
            pl.BlockSpec((tm, y.shape[1]), lambda i: (i, 0)),
            pl.BlockSpec((tm, d), lambda i: (i, 0)),
            pl.BlockSpec((None, 6, d), lambda i: (i // per_b, 0, 0)),
            pl.BlockSpec(w_out.shape, const),
            pl.BlockSpec((1, d), const),
            pl.BlockSpec((1, d), const),
        ],
        out_specs=pl.BlockSpec((tm, d), lambda i: (i, 0)),
        compiler_params=_cparams("arbitrary"),
        name="out_proj_ln",
    )(y, x2, mod_l, w_out, ln_g, ln_b)


def _peer_scores_kernel(x_ref, mod_ref, wq_ref, keys_ref, s_ref):
    sh = mod_ref[3:4, :]
    sc = mod_ref[4:5, :]
    h = (x_ref[...] * (1.0 + sc) + sh).astype(BF16)
    q = jnp.dot(h, wq_ref[...], preferred_element_type=F32).astype(BF16)
    for g in range(2 * PEER_HEADS):
        qg = q[:, g * PEER_HALF:(g + 1) * PEER_HALF]
        s_ref[g] = lax.dot_general(keys_ref[g], qg, (((1,), (1,)), ((), ())), preferred_element_type=F32)


def _peer_scores(x2, mod_l, wq, keys_g, seq):
    t, d = x2.shape
    tm = ROW_TILE
    per_b = seq // tm
    ng = 2 * PEER_HEADS
    return pl.pallas_call(
        _peer_scores_kernel,
        out_shape=jax.ShapeDtypeStruct((ng, N_KEYS, t), F32),
        grid=(t // tm,),
        in_specs=[
            pl.BlockSpec((tm, d), lambda i: (i, 0)),
            pl.BlockSpec((None, 6, d), lambda i: (i // per_b, 0, 0)),
            pl.BlockSpec(wq.shape, lambda i: (0, 0)),
            pl.BlockSpec(keys_g.shape, lambda i: (0, 0, 0)),
        ],
        out_specs=pl.BlockSpec((ng, N_KEYS, tm), lambda i: (0, 0, i)),
        compiler_params=_cparams("arbitrary"),
        name="peer_scores",
    )(x2, mod_l, wq, keys_g)


TOPK_TOKENS = LANES


def _top16_rows(s, ids, big):
    vals, out_ids = [], []
    for _ in range(PEER_TOPK):
        m = jnp.max(s, axis=0, keepdims=True)
        idx = jnp.min(jnp.where(s == m, ids, big), axis=0, keepdims=True)
        vals.append(m)
        out_ids.append(idx)
        s = jnp.where(ids == idx, NEG_INF, s)
    return jnp.concatenate(vals, axis=0), jnp.concatenate(out_ids, axis=0)


def _select_rows(table, sel):
    out = jnp.zeros(sel.shape, table.dtype)
    for a in range(PEER_TOPK):
        out = jnp.where(sel == a, table[a:a + 1, :], out)
    return out


def _pair_candidates(v1, v2):
    sub = lax.broadcasted_iota(jnp.int32, (SUBLANES,) + v1.shape[1:], 0)
    cand, flat = [], []
    for b0 in (0, SUBLANES):
        cand.append(v1[0:1, :] + v2[b0:b0 + SUBLANES, :])
        flat.append(sub + b0)
    for a in range(1, SUBLANES):
        cand.append(v1[a:a + 1, :] + v2[0:SUBLANES, :])
        flat.append(sub + a * PEER_TOPK)
    cand.append(v1[SUBLANES:, :] + v2[0:1, :])
    flat.append((sub + SUBLANES) * PEER_TOPK)
    return jnp.concatenate(cand, axis=0), jnp.concatenate(flat, axis=0)


def _peer_topk_kernel(s_ref, e_ref, g_ref, et_ref, gt_ref):
    key_ids = lax.broadcasted_iota(jnp.int32, s_ref.shape[1:], 0)

    def head(hd):
        v1, i1 = _top16_rows(s_ref[2 * hd], key_ids, N_KEYS)
        v2, i2 = _top16_rows(s_ref[2 * hd + 1], key_ids, N_KEYS)
        cand, flat = _pair_candidates(v1, v2)
        cv, ci = _top16_rows(cand, flat, PEER_TOPK * PEER_TOPK)
        e1 = _select_rows(i1, ci // PEER_TOPK)
        e2 = _select_rows(i2, ci % PEER_TOPK)
        p = jnp.exp(cv - cv[0:1, :])
        gates = p / jnp.sum(p, axis=0, keepdims=True)
        rows = pl.ds(pl.multiple_of(hd * PEER_TOPK, PEER_TOPK), PEER_TOPK)
        et_ref[rows, :] = e1 * N_KEYS + e2
        gt_ref[rows, :] = gates

    for hd in range(PEER_HEADS):
        head(hd)
    e_ref[...] = et_ref[...].T
    g_ref[...] = gt_ref[...].T


def _peer_topk(s):
    ng, nk, t = s.shape
    tt = TOPK_TOKENS
    width = PEER_HEADS * PEER_TOPK
    return pl.pallas_call(
        _peer_topk_kernel,
        out_shape=(jax.ShapeDtypeStruct((t, width), jnp.int32), jax.ShapeDtypeStruct((t, width), F32)),
        grid=(t // tt,),
        in_specs=[pl.BlockSpec((ng, nk, tt), lambda i: (0, 0, i))],
        out_specs=(pl.BlockSpec((tt, width), lambda i: (i, 0)), pl.BlockSpec((tt, width), lambda i: (i, 0))),
        scratch_shapes=[pltpu.VMEM((width, tt), jnp.int32), pltpu.VMEM((width, tt), F32)],
        compiler_params=_cparams("arbitrary"),
        name="peer_topk",
    )(s)


PEER_PHASES = 4
PEER_PHASE_TOKENS = 4
PEER_TOKENS = PEER_PHASES * PEER_PHASE_TOKENS
PEER_PICKS = PEER_HEADS * PEER_TOPK


def _gelu_tanh(z):
    return 0.5 * z * (1.0 + jnp.tanh(math.sqrt(2.0 / math.pi) * (z + 0.044715 * (z * z * z))))


def _unpack_pair(w32):
    lo = lax.bitcast_convert_type(w32 << 16, F32)
    hi = lax.bitcast_convert_type(w32 & jnp.uint32(0xFFFF0000), F32)
    return lo, hi


def _sublane_sums(tiles):
    assert len(tiles) == SUBLANES
    sub = lax.broadcasted_iota(jnp.int32, tiles[0].shape, 0)
    dist = SUBLANES // 2
    while dist >= 1:
        keep_low = (sub & dist) == 0
        nxt = []
        for j in range(len(tiles) // 2):
            lo_t, hi_t = tiles[j], tiles[j + len(tiles) // 2]
            nxt.append(jnp.where(keep_low, lo_t + pltpu.roll(lo_t, SUBLANES - dist, 0),
                                 hi_t + pltpu.roll(hi_t, dist, 0)))
        tiles = nxt
        dist //= 2
    return tiles[0]


def _peer_mix_kernel(alpha, e_cur_ref, e_nxt_ref, gates_ref, x_ref, mod_ref, g_ref, b_ref, uv_ref,
                     o_ref, buf0, buf1, buf2, buf3, h_ref, ffn_ref, sem_ref):
    i = pl.program_id(0)
    n = pl.num_programs(0)
    d = x_ref.shape[1]
    bufs = (buf0, buf1, buf2, buf3)
    ptok = PEER_PHASE_TOKENS

    def start_row(e_ref, row, buf, sem, t, k):
        pltpu.make_async_copy(uv_ref.at[e_ref[row, k]], buf.at[t, k], sem).start(priority=k % 2)

    def wait_buf(p):
        pltpu.make_async_copy(bufs[p], bufs[p], sem_ref.at[p]).wait()

    @pl.when(i == 0)
    def _():
        for p in range(2):
            for t in range(ptok):
                for k in range(PEER_PICKS):
                    start_row(e_cur_ref, p * ptok + t, bufs[p], sem_ref.at[p], t, k)

    x = x_ref[...]
    h = x * (1.0 + mod_ref[4:5, :]) + mod_ref[3:4, :]
    for s in range(d // LANES):
        h_ref[:, s, :] = h[:, s * LANES:(s + 1) * LANES]
    lane = lax.broadcasted_iota(jnp.int32, (SUBLANES, PEER_PICKS), 1)
    sub = lax.broadcasted_iota(jnp.int32, (SUBLANES, PEER_PICKS), 0)

    for p in range(PEER_PHASES):
        q = (p + 2) % PEER_PHASES
        ahead_ref, ahead_row0 = (e_cur_ref, (p + 2) * ptok) if p + 2 < PEER_PHASES else (e_nxt_ref, q * ptok)
        wait_buf(p)
        for t in range(ptok):
            r = p * ptok + t
            h_lo = h_ref[r, 0:SUBLANES, :]
            h_hi = h_ref[r, SUBLANES:, :]
            grow = gates_ref[r:r + 1, :]
            y_lo = jnp.zeros((SUBLANES, LANES), F32)
            y_hi = jnp.zeros((SUBLANES, LANES), F32)
            for g in range(PEER_PICKS // SUBLANES):
                k0 = g * SUBLANES
                prods = []
                for j in range(SUBLANES):
                    u_lo, u_hi = _unpack_pair(bufs[p][t, k0 + j, 0:SUBLANES, :])
                    start_row(ahead_ref, ahead_row0 + t, bufs[q], sem_ref.at[q], t, k0 + j)
                    prods.append(u_lo * h_lo + u_hi * h_hi)
                a = jnp.sum(_sublane_sums(prods), axis=1, keepdims=True)
                gate = jnp.sum(jnp.where(lane == sub + k0, grow, 0.0), axis=1, keepdims=True)
                w = jnp.broadcast_to(gate * _gelu_tanh(a), (SUBLANES, LANES))
                for j in range(SUBLANES):
                    v_lo, v_hi = _unpack_pair(bufs[p][t, k0 + j, SUBLANES:, :])
                    wj = jnp.broadcast_to(w[j:j + 1, :], (SUBLANES, LANES))
                    y_lo = y_lo + wj * v_lo
                    y_hi = y_hi + wj * v_hi
            ffn_ref[r, 0:SUBLANES, :] = y_lo
            ffn_ref[r, SUBLANES:, :] = y_hi

    @pl.when(i == n - 1)
    def _():
        wait_buf(0)
        wait_buf(1)

    ffn = jnp.concatenate([ffn_ref[:, s, :] for s in range(d // LANES)], axis=1)
    g2 = mod_ref[5:6, :]
    o_ref[...] = _layer_norm(alpha * x + (1.0 + g2) * ffn, g_ref[...], b_ref[...])


def _peer_mix(experts, gates, x2, mod_l, ln_g, ln_b, uvp, seq, alpha):
    t, d = x2.shape
    tb = PEER_TOKENS
    nblk = t // tb
    per_b = seq // tb
    const = lambda i: (0, 0)
    return pl.pallas_call(
        functools.partial(_peer_mix_kernel, alpha),
        out_shape=jax.ShapeDtypeStruct((t, d), F32),
        grid=(nblk,),
        in_specs=[
            pl.BlockSpec((tb, PEER_PICKS), lambda i: (i, 0), memory_space=pltpu.SMEM),
            pl.BlockSpec((tb, PEER_PICKS), lambda i: (jnp.minimum(i + 1, nblk - 1), 0),
                         memory_space=pltpu.SMEM),
            pl.BlockSpec((tb, PEER_PICKS), lambda i: (i, 0)),
            pl.BlockSpec((tb, d), lambda i: (i, 0)),
            pl.BlockSpec((None, 6, d), lambda i: (i // per_b, 0, 0)),
            pl.BlockSpec((1, d), const),
            pl.BlockSpec((1, d), const),
            pl.BlockSpec(memory_space=pl.ANY),
        ],
        out_specs=pl.BlockSpec((tb, d), lambda i: (i, 0)),
        scratch_shapes=[pltpu.VMEM((PEER_PHASE_TOKENS, PEER_PICKS, d // LANES, LANES), jnp.uint32)] * PEER_PHASES + [
                        pltpu.VMEM((tb, d // LANES, LANES), F32),
                        pltpu.VMEM((tb, d // LANES, LANES), F32),
                        pltpu.SemaphoreType.DMA((PEER_PHASES,))],
        compiler_params=_cparams("arbitrary"),
        name="peer_mix",
    )(experts, experts, gates, x2, mod_l, ln_g, ln_b, uvp)


PACK_ROWS = 512


def _bf16_bits_high(a):
    b = lax.bitcast_convert_type(a, jnp.uint32)
    return (b + jnp.uint32(0x7FFF) + ((b >> 16) & jnp.uint32(1))) & jnp.uint32(0xFFFF0000)


def _pack_table_kernel(u_ref, v_ref, o_ref):
    hd = u_ref.shape[1] // 2
    tiles = hd // LANES
    for part, src in enumerate((u_ref, v_ref)):
        words = (_bf16_bits_high(src[:, 0:hd]) >> 16) | _bf16_bits_high(src[:, hd:])
        for s in range(tiles):
            o_ref[:, part * tiles + s, :] = words[:, s * LANES:(s + 1) * LANES]


def _pack_table(peer_u, peer_v, layer):
    _, n, d = peer_u.shape
    spec = pl.BlockSpec((None, PACK_ROWS, d), lambda i: (layer, i, 0))
    return pl.pallas_call(
        _pack_table_kernel,
        out_shape=jax.ShapeDtypeStruct((n, d // LANES, LANES), jnp.uint32),
        grid=(n // PACK_ROWS,),
        in_specs=[spec, spec],
        out_specs=pl.BlockSpec((PACK_ROWS, d // LANES, LANES), lambda i: (i, 0, 0)),
        compiler_params=_cparams("arbitrary"),
        name="pack_table",
    )(peer_u, peer_v)


def kernel(x, c, ada_w, ada_b, w_in, w_gate2, b_gate, gla_norm_g, conv_w, conv_norm_g, w_out, ln1_g, ln1_b,
           peer_wq, peer_keys, peer_u, peer_v, ln2_g, ln2_b):
    bsz, seq, d = x.shape
    depth = ada_w.shape[0]
    alpha = (2.0 * depth) ** 0.25
    t = bsz * seq

    mod = _ada_mod(c, ada_w, ada_b).reshape(depth, bsz, 6, d)

    w_in_p = _w_in_prep(w_in)
    wg_p = jnp.pad(w_gate2, ((0, 0), (0, ALR_PAD - GLA_GATE_RANK), (0, 0)))
    w_out_b = w_out.astype(BF16)
    wq_b = peer_wq.astype(BF16)
    keys_g = peer_keys.reshape(depth, 2 * PEER_HEADS, N_KEYS, PEER_HALF).astype(BF16)
    uv = [_pack_table(peer_u, peer_v, l) for l in range(depth)]

    x2 = x.reshape(t, d)
    for l in range(depth):
        proj = _in_proj(x2, mod[l], w_in_p[l], seq)
        y = _gla_conv(proj, wg_p[l], b_gate[l][None], gla_norm_g[l][None], conv_w[l], conv_norm_g[l][None],
                      bsz, seq)
        x2 = _out_proj_ln(y, x2, mod[l], w_out_b[l], ln1_g[l][None], ln1_b[l][None], seq, alpha)
        s = _peer_scores(x2, mod[l], wq_b[l], keys_g[l], seq)
        experts, gates = _peer_topk(s)
        x2 = _peer_mix(experts, gates, x2, mod[l], ln2_g[l][None], ln2_b[l][None], uv[l], seq, alpha)
    return x2.reshape(bsz, seq, d)
```
